```python
import math
import jax, jax.numpy as jnp
from jax import lax
import numpy as np

D_MODEL = 2048
BATCH = 4
SEQ = 2048
DEPTH = 4
DEC_BATCH = 128
DEC_SEQ = 4
PAST_LEN = 16384
PAGE_SIZE = 128

EXPAND = 2
D_INNER = EXPAND * D_MODEL
N_AB = (DEPTH + 1) // 2
N_C = DEPTH // 2
CHUNK = 64
EPS = 1e-6
GLA_WIDTH = D_INNER // 2
GLA_HEADS = 4
GLA_DK = GLA_WIDTH // 2 // GLA_HEADS
GLA_DV = GLA_WIDTH // GLA_HEADS
GLA_QK = GLA_HEADS * GLA_DK
GLA_LOWRANK = 16
GLA_GATE_TEMP = 16.0
ML_WIDTH = D_INNER // 2
ML_HEADS = 4
ML_DK = ML_WIDTH // 2 // ML_HEADS
ML_DV = ML_WIDTH // ML_HEADS
ML_QK = ML_HEADS * ML_DK
ML_CONV = 4
AB_SPLITS = (GLA_QK, GLA_QK, GLA_WIDTH, GLA_WIDTH, GLA_LOWRANK,
             ML_QK, ML_QK, ML_WIDTH, ML_WIDTH, ML_WIDTH, ML_HEADS, ML_HEADS)
AB_IN = 2 * GLA_QK + 2 * GLA_WIDTH + GLA_LOWRANK + 2 * ML_QK + 3 * ML_WIDTH + 2 * ML_HEADS
SSD_HEAD_DIM = 64
SSD_HEADS = D_INNER // SSD_HEAD_DIM
SSD_GROUPS = 8
SSD_HPG = SSD_HEADS // SSD_GROUPS
SSD_STATE = 128
SSD_CONV = 4
SSD_CONV_DIM = D_INNER + 2 * SSD_GROUPS * SSD_STATE
C_IN = D_INNER + SSD_CONV_DIM + SSD_HEADS

kernel_name = 'hybrid_gla_mlstm_ssd_decode_step'

F32 = jnp.float32


def _rms(x):
    return x * lax.rsqrt(jnp.mean(jnp.square(x), axis=-1, keepdims=True) + EPS)


def rmsnorm(x, w):
    return (_rms(x.astype(F32)) * w.astype(F32)).astype(x.dtype)


def _split(t, sizes):
    out, o = [], 0
    for s in sizes:
        out.append(t[..., o:o + s])
        o += s
    return out


def to_chunks(t, c):
    b, l = t.shape[:2]
    return jnp.swapaxes(t.reshape((b, l // c, c) + t.shape[2:]), 0, 1)


def from_chunks(t):
    n, b, c = t.shape[:3]
    return jnp.swapaxes(t, 0, 1).reshape((b, n * c) + t.shape[3:])


def causal_conv(x, buf, w, b):
    k_w = w.shape[0]
    L = x.shape[1]
    xp = jnp.concatenate([buf.astype(x.dtype), x], axis=1)
    out = xp[:, 0:L] * w[0]
    for t in range(1, k_w):
        out = out + xp[:, t:t + L] * w[t]
    return out + b, xp[:, L:]


def gla_scan(q, k, v, g, s0):
    L = q.shape[1]
    c = math.gcd(L, CHUNK)
    q = q * (q.shape[-1] ** -0.5)
    mask = jnp.tril(jnp.ones((c, c), dtype=bool))

    def step(s, inp):
        qc, kc, vc, gc = inp
        b = jnp.cumsum(gc, axis=1)
        mid = b[:, c // 2][:, None]
        b_last = b[:, -1]
        o_inter = jnp.einsum('bihk,bhkv->bihv', qc * jnp.exp(b), s)
        att = jnp.einsum('bihk,bjhk->bhij', qc * jnp.exp(b - mid), kc * jnp.exp(mid - b))
        att = jnp.where(mask, att, 0.0)
        o_intra = jnp.einsum('bhij,bjhv->bihv', att, vc)
        s_new = jnp.exp(b_last)[..., None] * s + jnp.einsum(
            'bjhk,bjhv->bhkv', kc * jnp.exp(b_last[:, None] - b), vc)
        return s_new, o_inter + o_intra

    s_fin, o = lax.scan(step, s0, tuple(to_chunks(t, c) for t in (q, k, v, g)))
    return from_chunks(o), s_fin


def mlstm_scan(q, k, v, ig, fg, c0, n0, m0):
    L = q.shape[1]
    c = math.gcd(L, CHUNK)
    k = k * (k.shape[-1] ** -0.5)
    logf = jax.nn.log_sigmoid(fg)
    mask = jnp.tril(jnp.ones((c, c), dtype=bool))

    def step(carry, inp):
        cm, nm, mm = carry
        qc, kc, vc, ic, lfc = inp
        fcum = jnp.swapaxes(jnp.cumsum(lfc, axis=1), 1, 2)
        it = jnp.swapaxes(ic, 1, 2)
        dlog = jnp.where(mask, fcum[..., :, None] - fcum[..., None, :] + it[..., None, :], -jnp.inf)
        inter = fcum + mm[..., None]
        m_i = jnp.maximum(inter, jnp.max(dlog, axis=-1))
        w_inter = jnp.exp(inter - m_i)
        qk = jnp.einsum('bihk,bjhk->bhij', qc, kc) * jnp.exp(dlog - m_i[..., None])
        num = jnp.einsum('bhij,bjhv->bihv', qk, vc) + jnp.swapaxes(w_inter, 1, 2)[..., None] * \
            jnp.einsum('bihk,bhkv->bihv', qc, cm)
        den = jnp.sum(qk, axis=-1) + w_inter * jnp.einsum('bihk,bhk->bhi', qc, nm)
        den = jnp.maximum(jnp.abs(den), jnp.exp(-m_i))
        h = num / jnp.swapaxes(den, 1, 2)[..., None]
        m_new = m_i[..., -1]
        w_j = jnp.exp(fcum[..., -1:] - fcum + it - m_new[..., None])
        decay = jnp.exp(fcum[..., -1] + mm - m_new)
        c_new = decay[..., None, None] * cm + jnp.einsum('bhj,bjhk,bjhv->bhkv', w_j, kc, vc)
        n_new = decay[..., None] * nm + jnp.einsum('bhj,bjhk->bhk', w_j, kc)
        return (c_new, n_new, m_new), h

    (c_f, n_f, m_f), h = lax.scan(step, (c0, n0, m0),
                                  tuple(to_chunks(t, c) for t in (q, k, v, ig, logf)))
    return from_chunks(h), c_f, n_f, m_f


def ssd_scan(x, dt, a, bm, cm, s0):
    L = x.shape[1]
    c = math.gcd(L, CHUNK)
    mask = jnp.tril(jnp.ones((c, c), dtype=bool))[None, :, :, None, None]

    def step(s, inp):
        xc, dtc, bc, cc = inp
        acum = jnp.cumsum(dtc * a, axis=1)
        seg = acum[:, :, None] - acum[:, None, :]
        decay = jnp.exp(jnp.where(mask, seg, -jnp.inf))
        cb = jnp.einsum('bign,bjgn->bijg', cc, bc)
        y_intra = jnp.einsum('bijgh,bjghp->bighp', cb[..., None] * decay, dtc[..., None] * xc)
        y_inter = jnp.einsum('bign,bghpn->bighp', cc, s) * jnp.exp(acum)[..., None]
        a_last = acum[:, -1]
        w = jnp.exp(a_last[:, None] - acum) * dtc
        s_new = jnp.exp(a_last)[..., None, None] * s + jnp.einsum('bjgh,bjghp,bjgn->bghpn', w, xc, bc)
        return s_new, y_intra + y_inter

    s_fin, y = lax.scan(step, s0, tuple(to_chunks(t, c) for t in (x, dt, bm, cm)))
    return from_chunks(y), s_fin


def ab_mixer(h, s_gla, s_mc, s_mn, s_mm, s_mconv, w_in, gla_w_a2, gla_b_a, gla_norm,
             ml_conv_w, ml_conv_b, ml_b_i, ml_b_f, ml_norm, w_out):
    bsz, L, _ = h.shape
    proj = h @ w_in
    qg, kg, vg, zg, ag, qm, km, vm, om, zm, im, fm = _split(proj, AB_SPLITS)
    gate = jax.nn.log_sigmoid((ag @ gla_w_a2 + gla_b_a).astype(F32)) / GLA_GATE_TEMP
    hd = lambda t, d: t.astype(F32).reshape(bsz, L, -1, d)
    o_g, s_gla_new = gla_scan(hd(qg, GLA_DK), hd(kg, GLA_DK), hd(vg, GLA_DV), hd(gate, GLA_DK),
                              s_gla.astype(F32))
    o_g = (_rms(o_g) * gla_norm.astype(F32).reshape(GLA_HEADS, GLA_DV)).reshape(bsz, L, GLA_WIDTH)
    o_g = o_g * jax.nn.silu(zg.astype(F32))
    qk_c, mconv_new = causal_conv(jnp.concatenate([qm, km], axis=-1), s_mconv, ml_conv_w, ml_conv_b)
    qk_c = jax.nn.silu(qk_c.astype(F32))
    q_m, k_m = _split(qk_c, (ML_QK, ML_QK))
    h_m, c_new, n_new, m_new = mlstm_scan(
        hd(q_m, ML_DK), hd(k_m, ML_DK), hd(vm, ML_DV),
        (im + ml_b_i).astype(F32), (fm + ml_b_f).astype(F32),
        s_mc.astype(F32), s_mn.astype(F32), s_mm.astype(F32))
    h_m = jax.nn.sigmoid(hd(om, ML_DV)) * h_m
    h_m = (_rms(h_m) * ml_norm.astype(F32).reshape(ML_HEADS, ML_DV)).reshape(bsz, L, ML_WIDTH)
    h_m = h_m * jax.nn.silu(zm.astype(F32))
    y = jnp.concatenate([o_g, h_m], axis=-1).astype(h.dtype) @ w_out
    return y, s_gla_new, c_new, n_new, m_new, mconv_new


def c_mixer(h, s_ssm, s_conv, w_in, conv_w, conv_b, dt_bias, a_log, d_skip, norm_w, w_out):
    bsz, L, _ = h.shape
    proj = h @ w_in
    z, xbc, dt = _split(proj, (D_INNER, SSD_CONV_DIM, SSD_HEADS))
    xbc, conv_new = causal_conv(xbc, s_conv, conv_w, conv_b)
    xbc = jax.nn.silu(xbc.astype(F32))
    xs, bm, cm = _split(xbc, (D_INNER, SSD_GROUPS * SSD_STATE, SSD_GROUPS * SSD_STATE))
    xs = xs.reshape(bsz, L, SSD_GROUPS, SSD_HPG, SSD_HEAD_DIM)
    bm = bm.reshape(bsz, L, SSD_GROUPS, SSD_STATE)
    cm = cm.reshape(bsz, L, SSD_GROUPS, SSD_STATE)
    dt = jax.nn.softplus(dt.astype(F32) + dt_bias.astype(F32)).reshape(bsz, L, SSD_GROUPS, SSD_HPG)
    a = -jnp.exp(a_log.astype(F32)).reshape(SSD_GROUPS, SSD_HPG)
    s0 = s_ssm.astype(F32).reshape(bsz, SSD_GROUPS, SSD_HPG, SSD_HEAD_DIM, SSD_STATE)
    y, s_new = ssd_scan(xs, dt, a, bm, cm, s0)
    y = y + d_skip.astype(F32).reshape(SSD_GROUPS, SSD_HPG)[..., None] * xs
    y = y.reshape(bsz, L, D_INNER) * jax.nn.silu(z.astype(F32))
    y = _rms(y.reshape(bsz, L, SSD_GROUPS, D_INNER // SSD_GROUPS)).reshape(bsz, L, D_INNER)
    y = (y * norm_w.astype(F32)).astype(h.dtype) @ w_out
    return y, s_new.reshape(bsz, SSD_HEADS, SSD_HEAD_DIM, SSD_STATE), conv_new


def trunk(x, s_gla, s_mc, s_mn, s_mm, s_mconv, s_ssm, s_sconv, p):
    gla_l, mc_l, mn_l, mm_l, mcv_l, ssm_l, scv_l = [], [], [], [], [], [], []
    for layer in range(DEPTH):
        i = layer // 2
        if layer % 2 == 0:
            y, sg, sc, sn, sm, scv = ab_mixer(
                rmsnorm(x, p['norm_ab'][i]), s_gla[i], s_mc[i], s_mn[i], s_mm[i], s_mconv[i],
                p['w_in_ab'][i], p['gla_w_a2'][i], p['gla_b_a'][i], p['gla_norm'][i],
                p['mlstm_conv_w'][i], p['mlstm_conv_b'][i], p['mlstm_b_i'][i], p['mlstm_b_f'][i],
                p['mlstm_norm'][i], p['w_out_ab'][i])
            gla_l.append(sg); mc_l.append(sc); mn_l.append(sn); mm_l.append(sm); mcv_l.append(scv)
        else:
            y, ss, sv = c_mixer(
                rmsnorm(x, p['norm_c'][i]), s_ssm[i], s_sconv[i], p['w_in_c'][i],
                p['ssd_conv_w'][i], p['ssd_conv_b'][i], p['ssd_dt_bias'][i], p['ssd_a_log'][i],
                p['ssd_d'][i], p['ssd_norm'][i], p['w_out_c'][i])
            ssm_l.append(ss); scv_l.append(sv)
        x = x + y
    return (rmsnorm(x, p['final_norm']), jnp.stack(gla_l), jnp.stack(mc_l), jnp.stack(mn_l),
            jnp.stack(mm_l), jnp.stack(mcv_l), jnp.stack(ssm_l), jnp.stack(scv_l))


def setup_inputs(seed: int = 0) -> dict:
    key = jax.random.key(seed)
    ks = jax.random.split(key, 32)
    nrm = lambda k, shape, scale: jax.random.normal(k, shape, F32) * scale
    dt0 = jnp.exp(jax.random.uniform(ks[25], (N_C, SSD_HEADS), F32) * (math.log(0.1) - math.log(0.001))
                  + math.log(0.001))
    return {
        'x_prompt': nrm(ks[0], (BATCH, SEQ, D_MODEL), 1.0),
        'x_sample': nrm(ks[1], (DEC_BATCH, DEC_SEQ, D_MODEL), 1.0),
        'state_gla': nrm(ks[2], (N_AB, DEC_BATCH, GLA_HEADS, GLA_DK, GLA_DV), 0.1),
        'state_mlstm_c': nrm(ks[3], (N_AB, DEC_BATCH, ML_HEADS, ML_DK, ML_DV), 0.1),
        'state_mlstm_n': nrm(ks[4], (N_AB, DEC_BATCH, ML_HEADS, ML_DK), 0.5),
        'state_mlstm_m': jax.random.uniform(ks[5], (N_AB, DEC_BATCH, ML_HEADS), F32, 0.0, 4.0),
        'state_mlstm_conv': nrm(ks[6], (N_AB, DEC_BATCH, ML_CONV - 1, 2 * ML_QK), 1.0),
        'state_ssm': nrm(ks[7], (N_C, DEC_BATCH, SSD_HEADS, SSD_HEAD_DIM, SSD_STATE), 0.1),
        'state_ssm_conv': nrm(ks[8], (N_C, DEC_BATCH, SSD_CONV - 1, SSD_CONV_DIM), 1.0),
        'norm_ab': 1.0 + nrm(ks[9], (N_AB, D_MODEL), 0.02),
        'w_in_ab': nrm(ks[10], (N_AB, D_MODEL, AB_IN), D_MODEL ** -0.5),
        'gla_w_a2': nrm(ks[11], (N_AB, GLA_LOWRANK, GLA_QK), GLA_LOWRANK ** -0.5),
        'gla_b_a': nrm(ks[12], (N_AB, GLA_QK), 0.1),
        'gla_norm': 1.0 + nrm(ks[13], (N_AB, GLA_WIDTH), 0.02),
        'mlstm_conv_w': nrm(ks[14], (N_AB, ML_CONV, 2 * ML_QK), ML_CONV ** -0.5),
        'mlstm_conv_b': nrm(ks[15], (N_AB, 2 * ML_QK), 0.02),
        'mlstm_b_i': nrm(ks[16], (N_AB, ML_HEADS), 0.1),
        'mlstm_b_f': jnp.linspace(3.0, 6.0, ML_HEADS, dtype=F32) + nrm(ks[17], (N_AB, ML_HEADS), 0.1),
        'mlstm_norm': 1.0 + nrm(ks[18], (N_AB, ML_WIDTH), 0.02),
        'w_out_ab': nrm(ks[19], (N_AB, D_INNER, D_MODEL), D_INNER ** -0.5),
        'norm_c': 1.0 + nrm(ks[20], (N_C, D_MODEL), 0.02),
        'w_in_c': nrm(ks[21], (N_C, D_MODEL, C_IN), D_MODEL ** -0.5),
        'ssd_conv_w': nrm(ks[22], (N_C, SSD_CONV, SSD_CONV_DIM), SSD_CONV ** -0.5),
        'ssd_conv_b': nrm(ks[23], (N_C, SSD_CONV_DIM), 0.02),
        'ssd_dt_bias': dt0 + jnp.log(-jnp.expm1(-dt0)),
        'ssd_a_log': jnp.log(jax.random.uniform(ks[24], (N_C, SSD_HEADS), F32, 1.0, 16.0)),
        'ssd_d': 1.0 + nrm(ks[26], (N_C, SSD_HEADS), 0.1),
        'ssd_norm': 1.0 + nrm(ks[27], (N_C, D_INNER), 0.02),
        'w_out_c': nrm(ks[28], (N_C, D_INNER, D_MODEL), D_INNER ** -0.5),
        'final_norm': 1.0 + nrm(ks[29], (D_MODEL,), 0.02),
    }


def reference(x_prompt, x_sample, state_gla, state_mlstm_c, state_mlstm_n, state_mlstm_m,
              state_mlstm_conv, state_ssm, state_ssm_conv, norm_ab, w_in_ab, gla_w_a2, gla_b_a,
              gla_norm, mlstm_conv_w, mlstm_conv_b, mlstm_b_i, mlstm_b_f, mlstm_norm, w_out_ab,
              norm_c, w_in_c, ssd_conv_w, ssd_conv_b, ssd_dt_bias, ssd_a_log, ssd_d, ssd_norm,
              w_out_c, final_norm):
    p = dict(norm_ab=norm_ab, w_in_ab=w_in_ab, gla_w_a2=gla_w_a2, gla_b_a=gla_b_a, gla_norm=gla_norm,
             mlstm_conv_w=mlstm_conv_w, mlstm_conv_b=mlstm_conv_b, mlstm_b_i=mlstm_b_i,
             mlstm_b_f=mlstm_b_f, mlstm_norm=mlstm_norm, w_out_ab=w_out_ab, norm_c=norm_c,
             w_in_c=w_in_c, ssd_conv_w=ssd_conv_w, ssd_conv_b=ssd_conv_b, ssd_dt_bias=ssd_dt_bias,
             ssd_a_log=ssd_a_log, ssd_d=ssd_d, ssd_norm=ssd_norm, w_out_c=w_out_c,
             final_norm=final_norm)
    bp = x_prompt.shape[0]
    y_prompt, gla_p, mc_p, mn_p, mm_p, mcv_p, ssm_p, scv_p = trunk(
        x_prompt,
        jnp.zeros((N_AB, bp, GLA_HEADS, GLA_DK, GLA_DV), F32),
        jnp.zeros((N_AB, bp, ML_HEADS, ML_DK, ML_DV), F32),
        jnp.zeros((N_AB, bp, ML_HEADS, ML_DK), F32),
        jnp.zeros((N_AB, bp, ML_HEADS), F32),
        jnp.zeros((N_AB, bp, ML_CONV - 1, 2 * ML_QK), x_prompt.dtype),
        jnp.zeros((N_C, bp, SSD_HEADS, SSD_HEAD_DIM, SSD_STATE), F32),
        jnp.zeros((N_C, bp, SSD_CONV - 1, SSD_CONV_DIM), x_prompt.dtype),
        p)
    y_sample, gla_s, mc_s, mn_s, mm_s, mcv_s, ssm_s, scv_s = trunk(
        x_sample, state_gla, state_mlstm_c, state_mlstm_n, state_mlstm_m, state_mlstm_conv,
        state_ssm, state_ssm_conv, p)
    return (y_prompt, y_sample, gla_p, gla_s, mc_p, mc_s, mn_p, mn_s, mm_p, mm_s,
            mcv_p, mcv_s, ssm_p, ssm_s, scv_p, scv_s)
```

```python
import functools

import jax
import jax.numpy as jnp
from jax import lax
from jax.experimental import pallas as pl
from jax.experimental.pallas import tpu as pltpu

F32 = jnp.float32
BF16 = jnp.bfloat16
HI = lax.Precision.HIGHEST

D_MODEL = 2048
D_INNER = 4096
CHUNK = 64
EPS = 1e-6
NEG = -1e30
HEADS = 4
DK = 256
DV = 512
QK = HEADS * DK
WIDTH = HEADS * DV
GLA_LOWRANK = 16
GLA_GATE_TEMP = 16.0
CONV = 4
SM_I = 16
SM_F = 20
SSD_HEADS = 64
SSD_P = 64
SSD_P_LOG2 = 6
SSD_G = 8
SSD_HPG = 8
SSD_N = 128
SSD_GW = SSD_HPG * SSD_P
SSD_CONV_DIM = D_INNER + 2 * SSD_G * SSD_N
SMALL = 128
SUBLANES = 8
HIST = CONV - 1

VMEM_LIMIT = 48 * 1024 * 1024


def _cparams(sem):
    return pltpu.CompilerParams(dimension_semantics=sem, vmem_limit_bytes=VMEM_LIMIT)


def _dot(a, b):
    return jnp.dot(a.astype(BF16), b.astype(BF16), preferred_element_type=F32)


def _dot_nt(a, b):
    return lax.dot_general(a.astype(BF16), b.astype(BF16), (((1,), (1,)), ((), ())),
                           preferred_element_type=F32)


def _dot_tn(a, b):
    return lax.dot_general(a.astype(BF16), b.astype(BF16), (((0,), (0,)), ((), ())),
                           preferred_element_type=F32)


def _dot_hi(a, b):
    return jnp.dot(a, b, precision=HI, preferred_element_type=F32)


def _dot_tn_hi(a, b):
    return lax.dot_general(a, b, (((0,), (0,)), ((), ())), precision=HI,
                           preferred_element_type=F32)


def _sigmoid(x):
    return 1.0 / (1.0 + jnp.exp(-x))


def _silu(x):
    return x * _sigmoid(x)


def _softplus(x):
    return jnp.maximum(x, 0.0) + jnp.log(1.0 + jnp.exp(-jnp.abs(x)))


def _log_sigmoid(x):
    return -_softplus(-x)


def _rms(x):
    return x * lax.rsqrt(jnp.mean(x * x, axis=-1, keepdims=True) + EPS)


def _tri(c):
    row = lax.broadcasted_iota(jnp.int32, (c, c), 0)
    col = lax.broadcasted_iota(jnp.int32, (c, c), 1)
    return col <= row


def _inproj_kernel(x_ref, nw_ref, w_ref, ws_ref, bs_ref, o_ref, os_ref, xn_ref):
    @pl.when(pl.program_id(1) == 0)
    def _():
        xn = _rms(x_ref[...]) * nw_ref[...]
        xn_ref[...] = xn.astype(BF16)
        os_ref[...] = _dot_hi(xn, ws_ref[...]) + bs_ref[...]

    o_ref[...] = jnp.dot(xn_ref[...], w_ref[...], preferred_element_type=F32)


def _inproj(x, norm_w, w_main, w_small, b_small, *, tm, tn):
    t, d = x.shape
    n = w_main.shape[1]
    return pl.pallas_call(
        _inproj_kernel,
        out_shape=(jax.ShapeDtypeStruct((t, n), F32), jax.ShapeDtypeStruct((t, SMALL), F32)),
        grid=(t // tm, n // tn),
        in_specs=[
            pl.BlockSpec((tm, d), lambda i, j: (i, 0)),
            pl.BlockSpec((1, d), lambda i, j: (0, 0)),
            pl.BlockSpec((d, tn), lambda i, j: (0, j)),
            pl.BlockSpec((d, SMALL), lambda i, j: (0, 0)),
            pl.BlockSpec((1, SMALL), lambda i, j: (0, 0)),
        ],
        out_specs=(pl.BlockSpec((tm, tn), lambda i, j: (i, j)),
                   pl.BlockSpec((tm, SMALL), lambda i, j: (i, 0))),
        scratch_shapes=[pltpu.VMEM((tm, d), BF16)],
        compiler_params=_cparams(("parallel", "arbitrary")),
        name="inproj",
    )(x, norm_w, w_main, w_small, b_small)


def _outproj_kernel(a_ref, b_ref, wa_ref, wb_ref, x_ref, o_ref):
    o_ref[...] = (x_ref[...]
                  + jnp.dot(a_ref[...], wa_ref[...], preferred_element_type=F32)
                  + jnp.dot(b_ref[...], wb_ref[...], preferred_element_type=F32))


def _outproj(a, b, a_blk, b_blk, w, x, *, tm, tn):
    t, d = x.shape
    half = D_INNER // 2
    return pl.pallas_call(
        _outproj_kernel,
        out_shape=jax.ShapeDtypeStruct((t, d), F32),
        grid=(t // tm, d // tn),
        in_specs=[
            pl.BlockSpec((tm, half), lambda i, j: (i, a_blk)),
            pl.BlockSpec((tm, half), lambda i, j: (i, b_blk)),
            pl.BlockSpec((half, tn), lambda i, j: (0, j)),
            pl.BlockSpec((half, tn), lambda i, j: (1, j)),
            pl.BlockSpec((tm, tn), lambda i, j: (i, j)),
        ],
        out_specs=pl.BlockSpec((tm, tn), lambda i, j: (i, j)),
        compiler_params=_cparams(("parallel", "arbitrary")),
        name="outproj",
    )(a, b, w, w, x)


def _final_norm_kernel(x_ref, w_ref, o_ref):
    o_ref[...] = _rms(x_ref[...]) * w_ref[...]


def _final_norm(x, w, *, tm):
    t, d = x.shape
    return pl.pallas_call(
        _final_norm_kernel,
        out_shape=jax.ShapeDtypeStruct((t, d), F32),
        grid=(t // tm,),
        in_specs=[pl.BlockSpec((tm, d), lambda i: (i, 0)),
                  pl.BlockSpec((1, d), lambda i: (0, 0))],
        out_specs=pl.BlockSpec((tm, d), lambda i: (i, 0)),
        compiler_params=_cparams(("parallel",)),
        name="final_norm",
    )(x, w)


def _row_valid(c, valid):
    return lax.broadcasted_iota(jnp.int32, (c, 1), 0) < valid


def _col_valid(c, valid):
    return lax.broadcasted_iota(jnp.int32, (1, c), 1) < valid


def _causal_conv(buf_ref, x, w_ref, b_ref, c):
    buf_ref[pl.ds(SUBLANES, c), :] = x
    out = b_ref[...] + buf_ref[pl.ds(SUBLANES - HIST, c), :] * w_ref[0:1, :]
    for t in range(1, CONV):
        out = out + buf_ref[pl.ds(SUBLANES - HIST + t, c), :] * w_ref[t:t + 1, :]
    buf_ref[pl.ds(SUBLANES - HIST, HIST), :] = buf_ref[pl.ds(SUBLANES + c - HIST, HIST), :]
    return out


def _init_hist(buf_ref, hist):
    buf_ref[pl.ds(SUBLANES - HIST, HIST), :] = hist


def _gla_kernel(*refs, c, c_true, valid, has_init, has_prev):
    refs = list(refs)
    q_ref, k_ref, v_ref, z_ref, sm_ref, wa_ref, ba_ref, gn_ref = refs[:8]
    refs = refs[8:]
    s0_ref = refs.pop(0) if has_init else None
    if has_prev:
        refs.pop(0)
    o_ref, s_ref = refs

    @pl.when(pl.program_id(2) == 0)
    def _():
        if has_init:
            s_ref[0, 0, 0] = s0_ref[0, 0, 0]
        else:
            s_ref[0, 0, 0] = jnp.zeros((DK, DV), F32)

    q = q_ref[0] * (DK ** -0.5)
    k = k_ref[0]
    v = v_ref[0]
    g = _log_sigmoid(_dot_hi(sm_ref[0], wa_ref[...]) + ba_ref[...]) * (1.0 / GLA_GATE_TEMP)
    if valid < c:
        rv = _row_valid(c, valid)
        g = jnp.where(rv, g, 0.0)
        k = jnp.where(rv, k, 0.0)
    tri = _tri(c)
    b = _dot_hi(tri.astype(F32), g)
    mid = b[c_true // 2:c_true // 2 + 1, :]
    b_last = b[c - 1:c, :]
    s = s_ref[0, 0, 0]
    o = _dot(q * jnp.exp(b), s)
    att = _dot_nt(q * jnp.exp(b - mid), k * jnp.exp(mid - b))
    att = jnp.where(tri, att, 0.0)
    o = o + _dot(att, v)
    dcol = jnp.exp(_dot_tn_hi(g, jnp.ones((c, SMALL), F32)))
    s_ref[0, 0, 0] = (s * jnp.concatenate([dcol] * (DV // SMALL), axis=1)
                      + _dot_tn(k * jnp.exp(b_last - b), v))
    zz = z_ref[0]
    o_ref[0] = (_rms(o) * gn_ref[...] * _silu(zz)).astype(BF16)


def _gla(main, small, wa2, ba, gnorm, s0, prev, *, layer, n_layers, c, c_true, valid):
    bsz, L, _ = main.shape
    nc = L // c
    has_init = s0 is not None
    has_prev = prev is not None
    kern = functools.partial(_gla_kernel, c=c, c_true=c_true, valid=valid,
                             has_init=has_init, has_prev=has_prev)
    in_specs = [
        pl.BlockSpec((1, c, DK), lambda b, h, n: (b, n, h)),
        pl.BlockSpec((1, c, DK), lambda b, h, n: (b, n, HEADS + h)),
        pl.BlockSpec((1, c, DV), lambda b, h, n: (b, n, 2 * QK // DV + h)),
        pl.BlockSpec((1, c, DV), lambda b, h, n: (b, n, (2 * QK + WIDTH) // DV + h)),
        pl.BlockSpec((1, c, SMALL), lambda b, h, n: (b, n, 0)),
        pl.BlockSpec((SMALL, DK), lambda b, h, n: (0, h)),
        pl.BlockSpec((1, DK), lambda b, h, n: (0, h)),
        pl.BlockSpec((1, DV), lambda b, h, n: (0, h)),
    ]
    args = [main, main, main, main, small, wa2, ba, gnorm]
    if has_init:
        in_specs.append(pl.BlockSpec((1, 1, 1, DK, DV), lambda b, h, n: (layer, b, h, 0, 0)))
        args.append(s0)
    aliases = {}
    if has_prev:
        aliases = {len(args): 1}
        in_specs.append(pl.BlockSpec(memory_space=pl.ANY))
        args.append(prev)
    return pl.pallas_call(
        kern,
        out_shape=(jax.ShapeDtypeStruct((bsz, L, WIDTH), BF16),
                   jax.ShapeDtypeStruct((n_layers, bsz, HEADS, DK, DV), F32)),
        grid=(bsz, HEADS, nc),
        in_specs=in_specs,
        out_specs=(pl.BlockSpec((1, c, DV), lambda b, h, n: (b, n, h)),
                   pl.BlockSpec((1, 1, 1, DK, DV), lambda b, h, n: (layer, b, h, 0, 0))),
        input_output_aliases=aliases,
        compiler_params=_cparams(("parallel", "parallel", "arbitrary")),
        name="gla_scan",
    )(*args)


def _mlstm_kernel(*refs, c, valid, has_init, n_prev):
    refs = list(refs)
    (qp_ref, kp_ref, v_ref, og_ref, z_ref, sm_ref, gt_ref,
     cwq_ref, cwk_ref, cbq_ref, cbk_ref, mn_ref) = refs[:12]
    refs = refs[12:]
    if has_init:
        c0_ref, n0_ref, m0_ref, hq_ref, hk_ref = refs[:5]
        refs = refs[5:]
    refs = refs[n_prev:]
    o_ref, c_ref, n_ref, m_ref, qbuf, kbuf = refs
    h = pl.program_id(1)

    @pl.when(pl.program_id(2) == 0)
    def _():
        if has_init:
            c_ref[0, 0, 0] = c0_ref[0, 0, 0]
            n_ref[0, 0, 0] = n0_ref[0, 0, 0]
            m_ref[0, 0, 0] = jnp.broadcast_to(m0_ref[0, 0, 0], (1, SMALL))
            _init_hist(qbuf, hq_ref[0, 0])
            _init_hist(kbuf, hk_ref[0, 0])
        else:
            c_ref[0, 0, 0] = jnp.zeros((DK, DV), F32)
            n_ref[0, 0, 0] = jnp.zeros((1, DK), F32)
            m_ref[0, 0, 0] = jnp.zeros((1, SMALL), F32)
            _init_hist(qbuf, jnp.zeros((HIST, DK), F32))
            _init_hist(kbuf, jnp.zeros((HIST, DK), F32))

    q = _silu(_causal_conv(qbuf, qp_ref[0], cwq_ref, cbq_ref, c))
    k = _silu(_causal_conv(kbuf, kp_ref[0], cwk_ref, cbk_ref, c)) * (DK ** -0.5)
    v = v_ref[0]

    sm = sm_ref[0]
    lane = lax.broadcasted_iota(jnp.int32, (1, SMALL), 1)
    ig_c = jnp.sum(jnp.where(lane == SM_I + h, sm, 0.0), axis=1, keepdims=True)
    fg_c = jnp.sum(jnp.where(lane == SM_F + h, sm, 0.0), axis=1, keepdims=True)
    gt = gt_ref[0, 0]
    sub = lax.broadcasted_iota(jnp.int32, (2 * HEADS, 1), 0)
    ig_r = jnp.sum(jnp.where(sub == h, gt, 0.0), axis=0, keepdims=True)
    fg_r = jnp.sum(jnp.where(sub == HEADS + h, gt, 0.0), axis=0, keepdims=True)
    lf_c = _log_sigmoid(fg_c)
    lf_r = _log_sigmoid(fg_r)
    if valid < c:
        rv = _row_valid(c, valid)
        cv = _col_valid(c, valid)
        lf_c = jnp.where(rv, lf_c, 0.0)
        lf_r = jnp.where(cv, lf_r, 0.0)
        ig_c = jnp.where(rv, ig_c, NEG)
        ig_r = jnp.where(cv, ig_r, NEG)

    tri = _tri(c)
    row = lax.broadcasted_iota(jnp.int32, (c, c), 0)
    col = lax.broadcasted_iota(jnp.int32, (c, c), 1)
    fcum_c = jnp.sum(jnp.where(tri, lf_r, 0.0), axis=1, keepdims=True)
    fcum_r = jnp.sum(jnp.where(row <= col, lf_c, 0.0), axis=0, keepdims=True)
    m_prev = m_ref[0, 0, 0][:, 0:1]
    dlog = jnp.where(tri, fcum_c - fcum_r + ig_r, NEG)
    inter = fcum_c + m_prev
    m_i = jnp.maximum(inter, jnp.max(dlog, axis=1, keepdims=True))
    w_inter = jnp.exp(inter - m_i)
    qk = _dot_nt(q, k) * jnp.exp(dlog - m_i)
    cm = c_ref[0, 0, 0]
    nm = n_ref[0, 0, 0]
    num = _dot(qk, v) + w_inter * _dot(q, cm)
    den = jnp.sum(qk, axis=1, keepdims=True) + w_inter * jnp.sum(q * nm, axis=1, keepdims=True)
    den = jnp.maximum(jnp.abs(den), jnp.exp(-m_i))
    hh = num / den
    m_new = m_i[c - 1:c, :]
    f_last = fcum_c[c - 1:c, :]
    w_j = jnp.exp(f_last - fcum_c + ig_c - m_new)
    decay = jnp.exp(f_last + m_prev - m_new)
    kw = w_j * k
    c_ref[0, 0, 0] = decay * cm + _dot_tn(kw, v)
    n_ref[0, 0, 0] = decay * nm + jnp.sum(kw, axis=0, keepdims=True)
    m_ref[0, 0, 0] = jnp.broadcast_to(m_new, (1, SMALL))

    hm = _sigmoid(og_ref[0]) * hh
    o_ref[0] = (_rms(hm) * mn_ref[...] * _silu(z_ref[0])).astype(BF16)


def _mlstm(main, small, gates_t, cw, cb, mnorm, init, prev, *, layer, n_layers, c, valid):
    bsz, L, _ = main.shape
    nc = L // c
    has_init = init is not None
    n_prev = 0 if prev is None else len(prev)
    kern = functools.partial(_mlstm_kernel, c=c, valid=valid, has_init=has_init, n_prev=n_prev)
    base = (2 * QK + 2 * WIDTH)
    in_specs = [
        pl.BlockSpec((1, c, DK), lambda b, h, n: (b, n, base // DK + h)),
        pl.BlockSpec((1, c, DK), lambda b, h, n: (b, n, base // DK + HEADS + h)),
        pl.BlockSpec((1, c, DV), lambda b, h, n: (b, n, (base + 2 * QK) // DV + h)),
        pl.BlockSpec((1, c, DV), lambda b, h, n: (b, n, (base + 2 * QK + WIDTH) // DV + h)),
        pl.BlockSpec((1, c, DV), lambda b, h, n: (b, n, (base + 2 * QK + 2 * WIDTH) // DV + h)),
        pl.BlockSpec((1, c, SMALL), lambda b, h, n: (b, n, 0)),
        pl.BlockSpec((1, 1, 2 * HEADS, c), lambda b, h, n: (b, n, 0, 0)),
        pl.BlockSpec((CONV, DK), lambda b, h, n: (0, h)),
        pl.BlockSpec((CONV, DK), lambda b, h, n: (0, HEADS + h)),
        pl.BlockSpec((1, DK), lambda b, h, n: (0, h)),
        pl.BlockSpec((1, DK), lambda b, h, n: (0, HEADS + h)),
        pl.BlockSpec((1, DV), lambda b, h, n: (0, h)),
    ]
    args = [main, main, main, main, main, small, gates_t, cw, cw, cb, cb, mnorm]
    if has_init:
        c0, n0, m0, conv0 = init
        in_specs += [
            pl.BlockSpec((1, 1, 1, DK, DV), lambda b, h, n: (layer, b, h, 0, 0)),
            pl.BlockSpec((1, 1, 1, 1, DK), lambda b, h, n: (layer, b, h, 0, 0)),
            pl.BlockSpec((1, 1, 1, 1, 1), lambda b, h, n: (layer, b, h, 0, 0)),
            pl.BlockSpec((1, 1, HIST, DK), lambda b, h, n: (layer, b, 0, h)),
            pl.BlockSpec((1, 1, HIST, DK), lambda b, h, n: (layer, b, 0, HEADS + h)),
        ]
        args += [c0, n0, m0, conv0, conv0]
    aliases = {}
    if n_prev:
        for i, p in enumerate(prev):
            aliases[len(args)] = 1 + i
            in_specs.append(pl.BlockSpec(memory_space=pl.ANY))
            args.append(p)
    return pl.pallas_call(
        kern,
        out_shape=(jax.ShapeDtypeStruct((bsz, L, WIDTH), BF16),
                   jax.ShapeDtypeStruct((n_layers, bsz, HEADS, DK, DV), F32),
                   jax.ShapeDtypeStruct((n_layers, bsz, HEADS, 1, DK), F32),
                   jax.ShapeDtypeStruct((n_layers, bsz, HEADS, 1, SMALL), F32)),
        grid=(bsz, HEADS, nc),
        in_specs=in_specs,
        out_specs=(pl.BlockSpec((1, c, DV), lambda b, h, n: (b, n, h)),
                   pl.BlockSpec((1, 1, 1, DK, DV), lambda b, h, n: (layer, b, h, 0, 0)),
                   pl.BlockSpec((1, 1, 1, 1, DK), lambda b, h, n: (layer, b, h, 0, 0)),
                   pl.BlockSpec((1, 1, 1, 1, SMALL), lambda b, h, n: (layer, b, h, 0, 0))),
        scratch_shapes=[pltpu.VMEM((SUBLANES + c, DK), F32), pltpu.VMEM((SUBLANES + c, DK), F32)],
        input_output_aliases=aliases,
        compiler_params=_cparams(("parallel", "parallel", "arbitrary")),
        name="mlstm_scan",
    )(*args)


def _ssd_kernel(*refs, c, valid, has_init, has_prev):
    refs = list(refs)
    (z_ref, xp_ref, bp_ref, cp_ref, dtc_ref, dtr_ref, cwx_ref, cwb_ref, cwc_ref,
     cbx_ref, cbb_ref, cbc_ref, alr_ref, alc_ref, d_ref, nw_ref) = refs[:16]
    refs = refs[16:]
    if has_init:
        s0_ref, hx_ref, hb_ref, hc_ref = refs[:4]
        refs = refs[4:]
    if has_prev:
        refs = refs[1:]
    o_ref, s_ref, xbuf, bbuf, cbuf = refs

    @pl.when(pl.program_id(2) == 0)
    def _():
        if has_init:
            s_ref[0, 0, 0] = s0_ref[0, 0, 0]
            _init_hist(xbuf, hx_ref[0, 0])
            _init_hist(bbuf, hb_ref[0, 0])
            _init_hist(cbuf, hc_ref[0, 0])
        else:
            s_ref[0, 0, 0] = jnp.zeros((SSD_GW, SSD_N), F32)
            _init_hist(xbuf, jnp.zeros((HIST, SSD_GW), F32))
            _init_hist(bbuf, jnp.zeros((HIST, SSD_N), F32))
            _init_hist(cbuf, jnp.zeros((HIST, SSD_N), F32))

    x = _silu(_causal_conv(xbuf, xp_ref[0], cwx_ref, cbx_ref, c))
    bm = _silu(_causal_conv(bbuf, bp_ref[0], cwb_ref, cbb_ref, c))
    cm = _silu(_causal_conv(cbuf, cp_ref[0], cwc_ref, cbc_ref, c))

    dt_c = _softplus(dtc_ref[0, 0])
    dt_r = _softplus(dtr_ref[0, 0, 0])
    if valid < c:
        dt_c = jnp.where(_row_valid(c, valid), dt_c, 0.0)
        dt_r = jnp.where(_col_valid(c, valid), dt_r, 0.0)
    a_r = -jnp.exp(alr_ref[0])
    a_c = -jnp.exp(alc_ref[0])
    tri = _tri(c)
    row = lax.broadcasted_iota(jnp.int32, (c, c), 0)
    col = lax.broadcasted_iota(jnp.int32, (c, c), 1)
    acum_c = _dot_hi(tri.astype(F32), dt_c * a_r)
    acum_r = _dot_hi(dt_r * a_c, (row <= col).astype(F32))

    e = (lax.shift_right_logical(lax.broadcasted_iota(jnp.int32, (SSD_HPG, SSD_GW), 1), SSD_P_LOG2)
         == lax.broadcasted_iota(jnp.int32, (SSD_HPG, SSD_GW), 0)).astype(F32)
    et = (lax.shift_right_logical(lax.broadcasted_iota(jnp.int32, (SSD_GW, SSD_HPG), 0), SSD_P_LOG2)
          == lax.broadcasted_iota(jnp.int32, (SSD_GW, SSD_HPG), 1)).astype(F32)

    cb = _dot_nt(cm, bm)
    xdt = x * _dot_hi(dt_c, e)
    ys = []
    for hh in range(SSD_HPG):
        seg = acum_c[:, hh:hh + 1] - acum_r[hh:hh + 1, :]
        dec = jnp.exp(jnp.where(tri, seg, NEG))
        ys.append(_dot(cb * dec, xdt[:, hh * SSD_P:(hh + 1) * SSD_P]))
    y = jnp.concatenate(ys, axis=1)
    s = s_ref[0, 0, 0]
    y = y + _dot_nt(cm, s) * _dot_hi(jnp.exp(acum_c), e)
    a_last = acum_c[c - 1:c, :]
    w = jnp.exp(a_last - acum_c) * dt_c
    sdec = jnp.exp(_dot_hi(et, jnp.broadcast_to(acum_r[:, c - 1:c], (SSD_HPG, SSD_N))))
    s_ref[0, 0, 0] = s * sdec + _dot_tn(x * _dot_hi(w, e), bm)

    y = y + d_ref[0] * x
    y = y * _silu(z_ref[0])
    o_ref[0] = (_rms(y) * nw_ref[...]).astype(BF16)


def _ssd(main, dt_col, dt_row, cw, cb, alog_r, alog_c, d_exp, normw, init, prev,
         *, layer, n_layers, c, valid):
    bsz, L, _ = main.shape
    nc = L // c
    has_init = init is not None
    has_prev = prev is not None
    kern = functools.partial(_ssd_kernel, c=c, valid=valid, has_init=has_init, has_prev=has_prev)
    xb = D_INNER // SSD_GW
    bb = (2 * D_INNER) // SSD_N
    cb_ = bb + SSD_G
    in_specs = [
        pl.BlockSpec((1, c, SSD_GW), lambda b, g, n: (b, n, g)),
        pl.BlockSpec((1, c, SSD_GW), lambda b, g, n: (b, n, xb + g)),
        pl.BlockSpec((1, c, SSD_N), lambda b, g, n: (b, n, bb + g)),
        pl.BlockSpec((1, c, SSD_N), lambda b, g, n: (b, n, cb_ + g)),
        pl.BlockSpec((1, 1, c, SSD_HPG), lambda b, g, n: (b, g, n, 0)),
        pl.BlockSpec((1, 1, 1, SSD_HPG, c), lambda b, g, n: (b, g, n, 0, 0)),
        pl.BlockSpec((CONV, SSD_GW), lambda b, g, n: (0, g)),
        pl.BlockSpec((CONV, SSD_N), lambda b, g, n: (0, D_INNER // SSD_N + g)),
        pl.BlockSpec((CONV, SSD_N), lambda b, g, n: (0, D_INNER // SSD_N + SSD_G + g)),
        pl.BlockSpec((1, SSD_GW), lambda b, g, n: (0, g)),
        pl.BlockSpec((1, SSD_N), lambda b, g, n: (0, D_INNER // SSD_N + g)),
        pl.BlockSpec((1, SSD_N), lambda b, g, n: (0, D_INNER // SSD_N + SSD_G + g)),
        pl.BlockSpec((1, 1, SSD_HPG), lambda b, g, n: (g, 0, 0)),
        pl.BlockSpec((1, SSD_HPG, 1), lambda b, g, n: (g, 0, 0)),
        pl.BlockSpec((1, 1, SSD_GW), lambda b, g, n: (g, 0, 0)),
        pl.BlockSpec((1, SSD_GW), lambda b, g, n: (0, g)),
    ]
    args = [main, main, main, main, dt_col, dt_row, cw, cw, cw, cb, cb, cb,
            alog_r, alog_c, d_exp, normw]
    if has_init:
        s0, conv0 = init
        in_specs += [
            pl.BlockSpec((1, 1, 1, SSD_GW, SSD_N), lambda b, g, n: (layer, b, g, 0, 0)),
            pl.BlockSpec((1, 1, HIST, SSD_GW), lambda b, g, n: (layer, b, 0, g)),
            pl.BlockSpec((1, 1, HIST, SSD_N), lambda b, g, n: (layer, b, 0, D_INNER // SSD_N + g)),
            pl.BlockSpec((1, 1, HIST, SSD_N),
                         lambda b, g, n: (layer, b, 0, D_INNER // SSD_N + SSD_G + g)),
        ]
        args += [s0, conv0, conv0, conv0]
    aliases = {}
    if has_prev:
        aliases = {len(args): 1}
        in_specs.append(pl.BlockSpec(memory_space=pl.ANY))
        args.append(prev)
    return pl.pallas_call(
        kern,
        out_shape=(jax.ShapeDtypeStruct((bsz, L, D_INNER), BF16),
                   jax.ShapeDtypeStruct((n_layers, bsz, SSD_G, SSD_GW, SSD_N), F32)),
        grid=(bsz, SSD_G, nc),
        in_specs=in_specs,
        out_specs=(pl.BlockSpec((1, c, SSD_GW), lambda b, g, n: (b, n, g)),
                   pl.BlockSpec((1, 1, 1, SSD_GW, SSD_N), lambda b, g, n: (layer, b, g, 0, 0))),
        scratch_shapes=[pltpu.VMEM((SUBLANES + c, SSD_GW), F32),
                        pltpu.VMEM((SUBLANES + c, SSD_N), F32),
                        pltpu.VMEM((SUBLANES + c, SSD_N), F32)],
        input_output_aliases=aliases,
        compiler_params=_cparams(("parallel", "parallel", "arbitrary")),
        name="ssd_scan",
    )(*args)


def _prep_params(p):
    n_ab = p['w_in_ab'].shape[0]
    n_c = p['w_in_c'].shape[0]
    o_ag = 2 * QK + 2 * WIDTH
    o_m = o_ag + GLA_LOWRANK
    o_if = o_m + 2 * QK + 3 * WIDTH
    ab, cc = [], []
    for i in range(n_ab):
        w = p['w_in_ab'][i]
        w_main = jnp.concatenate([w[:, :o_ag], w[:, o_m:o_if]], axis=1).astype(BF16)
        w_small = jnp.zeros((D_MODEL, SMALL), F32)
        w_small = w_small.at[:, :GLA_LOWRANK].set(w[:, o_ag:o_m])
        w_small = w_small.at[:, SM_I:SM_I + 2 * HEADS].set(w[:, o_if:])
        b_small = jnp.zeros((1, SMALL), F32)
        b_small = b_small.at[0, SM_I:SM_I + HEADS].set(p['mlstm_b_i'][i])
        b_small = b_small.at[0, SM_F:SM_F + HEADS].set(p['mlstm_b_f'][i])
        wa2 = jnp.zeros((SMALL, QK), F32).at[:GLA_LOWRANK].set(p['gla_w_a2'][i])
        ab.append(dict(
            norm=p['norm_ab'][i][None], w_main=w_main, w_small=w_small, b_small=b_small,
            wa2=wa2, ba=p['gla_b_a'][i][None], gnorm=p['gla_norm'][i][None],
            cw=p['mlstm_conv_w'][i], cb=p['mlstm_conv_b'][i][None],
            mnorm=p['mlstm_norm'][i][None], w_out=p['w_out_ab'][i].astype(BF16)))
    o_dt = D_INNER + SSD_CONV_DIM
    for i in range(n_c):
        w = p['w_in_c'][i]
        w_small = jnp.zeros((D_MODEL, SMALL), F32).at[:, :SSD_HEADS].set(w[:, o_dt:])
        b_small = jnp.zeros((1, SMALL), F32).at[0, :SSD_HEADS].set(p['ssd_dt_bias'][i])
        cc.append(dict(
            norm=p['norm_c'][i][None], w_main=w[:, :o_dt].astype(BF16), w_small=w_small,
            b_small=b_small, cw=p['ssd_conv_w'][i], cb=p['ssd_conv_b'][i][None],
            alog_r=p['ssd_a_log'][i].reshape(SSD_G, 1, SSD_HPG),
            alog_c=p['ssd_a_log'][i].reshape(SSD_G, SSD_HPG, 1),
            d_exp=jnp.repeat(p['ssd_d'][i], SSD_P).reshape(SSD_G, 1, SSD_GW),
            normw=p['ssd_norm'][i][None], w_out=p['w_out_c'][i].astype(BF16)))
    return ab, cc


def _trunk(x, states, ab, cc, final_w, *, c, c_true, valid, tm):
    bsz, L, _ = x.shape
    t = bsz * L
    nc = L // c
    n_ab, n_c = len(ab), len(cc)
    xt = x.reshape(t, D_MODEL)
    gla_o = mc_o = mn_o = mm_o = ssm_o = None
    mconv, sconv = [], []
    lo = valid - HIST
    row0 = (nc - 1) * c + lo
    for layer in range(n_ab + n_c):
        i = layer // 2
        if layer % 2 == 0:
            P = ab[i]
            main, small = _inproj(xt, P['norm'], P['w_main'], P['w_small'], P['b_small'],
                                  tm=tm, tn=1024)
            main = main.reshape(bsz, L, -1)
            small = small.reshape(bsz, L, SMALL)
            gates_t = jnp.swapaxes(
                small[:, :, SM_I:SM_I + 2 * HEADS].reshape(bsz, nc, c, 2 * HEADS), 2, 3)
            s_gla = None if states is None else states['gla']
            og, gla_o = _gla(main, small, P['wa2'], P['ba'], P['gnorm'], s_gla, gla_o,
                             layer=i, n_layers=n_ab, c=c, c_true=c_true, valid=valid)
            init = None if states is None else (states['mc'], states['mn'], states['mm'],
                                                states['mconv'])
            prev = None if mc_o is None else (mc_o, mn_o, mm_o)
            om, mc_o, mn_o, mm_o = _mlstm(main, small, gates_t, P['cw'], P['cb'], P['mnorm'],
                                          init, prev, layer=i, n_layers=n_ab, c=c, valid=valid)
            mconv.append(main[:, row0:row0 + HIST, 2 * QK + 2 * WIDTH:2 * QK + 2 * WIDTH + 2 * QK])
            xt = _outproj(og.reshape(t, WIDTH), om.reshape(t, WIDTH), 0, 0, P['w_out'], xt,
                          tm=min(tm, 512), tn=512)
        else:
            P = cc[i]
            main, small = _inproj(xt, P['norm'], P['w_main'], P['w_small'], P['b_small'],
                                  tm=tm, tn=1024)
            main = main.reshape(bsz, L, -1)
            dt = small[:, :SSD_HEADS].reshape(bsz, nc, c, SSD_G, SSD_HPG)
            dt_col = jnp.transpose(dt, (0, 3, 1, 2, 4)).reshape(bsz, SSD_G, L, SSD_HPG)
            dt_row = jnp.transpose(dt, (0, 3, 1, 4, 2))
            init = None if states is None else (states['ssm'], states['sconv'])
            y, ssm_o = _ssd(main, dt_col, dt_row, P['cw'], P['cb'], P['alog_r'], P['alog_c'],
                            P['d_exp'], P['normw'], init, ssm_o,
                            layer=i, n_layers=n_c, c=c, valid=valid)
            sconv.append(main[:, row0:row0 + HIST, D_INNER:D_INNER + SSD_CONV_DIM])
            y2 = y.reshape(t, D_INNER)
            xt = _outproj(y2, y2, 0, 1, P['w_out'], xt, tm=min(tm, 512), tn=512)
    y = _final_norm(xt, final_w, tm=min(tm, 512)).reshape(bsz, L, D_MODEL)
    return (y, gla_o, mc_o, mn_o[:, :, :, 0, :], mm_o[:, :, :, 0, 0],
            jnp.stack(mconv), ssm_o.reshape(n_c, bsz, SSD_HEADS, SSD_P, SSD_N), jnp.stack(sconv))


def kernel(x_prompt, x_sample, state_gla, state_mlstm_c, state_mlstm_n, state_mlstm_m, state_mlstm_conv, state_ssm, state_ssm_conv, norm_ab, w_in_ab, gla_w_a2, gla_b_a, gla_norm, mlstm_conv_w, mlstm_conv_b, mlstm_b_i, mlstm_b_f, mlstm_norm, w_out_ab, norm_c, w_in_c, ssd_conv_w, ssd_conv_b, ssd_dt_bias, ssd_a_log, ssd_d, ssd_norm, w_out_c, final_norm):
    p = dict(norm_ab=norm_ab, w_in_ab=w_in_ab, gla_w_a2=gla_w_a2, gla_b_a=gla_b_a, gla_norm=gla_norm,
             mlstm_conv_w=mlstm_conv_w, mlstm_conv_b=mlstm_conv_b, mlstm_b_i=mlstm_b_i,
             mlstm_b_f=mlstm_b_f, mlstm_norm=mlstm_norm, w_out_ab=w_out_ab, norm_c=norm_c,
             w_in_c=w_in_c, ssd_conv_w=ssd_conv_w, ssd_conv_b=ssd_conv_b, ssd_dt_bias=ssd_dt_bias,
             ssd_a_log=ssd_a_log, ssd_d=ssd_d, ssd_norm=ssd_norm, w_out_c=w_out_c)
    ab, cc = _prep_params(p)
    fw = final_norm[None]
    n_ab, dec_b = state_gla.shape[0], state_gla.shape[1]
    n_c = state_ssm.shape[0]
    dec_l = x_sample.shape[1]

    yp, gla_p, mc_p, mn_p, mm_p, mcv_p, ssm_p, scv_p = _trunk(
        x_prompt, None, ab, cc, fw, c=CHUNK, c_true=CHUNK, valid=CHUNK, tm=512)

    xs = jnp.pad(x_sample, ((0, 0), (0, SUBLANES - dec_l), (0, 0)))
    states = dict(
        gla=state_gla, mc=state_mlstm_c,
        mn=state_mlstm_n.reshape(n_ab, dec_b, HEADS, 1, DK),
        mm=state_mlstm_m.reshape(n_ab, dec_b, HEADS, 1, 1),
        mconv=state_mlstm_conv,
        ssm=state_ssm.reshape(n_c, dec_b, SSD_G, SSD_GW, SSD_N),
        sconv=state_ssm_conv)
    ys, gla_s, mc_s, mn_s, mm_s, mcv_s, ssm_s, scv_s = _trunk(
        xs, states, ab, cc, fw, c=SUBLANES, c_true=dec_l, valid=dec_l, tm=512)
    ys = ys[:, :dec_l]
    return (yp, ys, gla_p, gla_s, mc_p, mc_s, mn_p, mn_s, mm_p, mm_s,
            mcv_p, mcv_s, ssm_p, ssm_s, scv_p, scv_s)
```

```python
import functools

import jax
import jax.numpy as jnp
from jax import lax
from jax.experimental import pallas as pl
from jax.experimental.pallas import tpu as pltpu

F32 = jnp.float32
BF16 = jnp.bfloat16

D_MODEL = 2048
D_INNER = 4096
CHUNK = 64
EPS = 1e-6
NEG = -1e30
HEADS = 4
DK = 256
DV = 512
QK = HEADS * DK
WIDTH = HEADS * DV
GLA_LOWRANK = 16
GLA_GATE_TEMP = 16.0
CONV = 4
HIST = CONV - 1
SM_I = 16
SM_F = 20
SSD_HEADS = 64
SSD_P = 64
SSD_G = 8
SSD_HPG = 8
SSD_N = 128
SSD_GW = SSD_HPG * SSD_P
SSD_BC = 2 * SSD_G * SSD_N
SSD_CONV_DIM = D_INNER + SSD_BC
LANES = 128
SUBLANES = 8
SLOT = 64
SLOT_LOG2 = 6

VMEM_LIMIT = 56 * 1024 * 1024


def _cparams(sem):
    return pltpu.CompilerParams(dimension_semantics=sem, vmem_limit_bytes=VMEM_LIMIT)


def _dot(a, b):
    return jnp.dot(a.astype(BF16), b.astype(BF16), preferred_element_type=F32)


def _dot_nt(a, b):
    return lax.dot_general(a.astype(BF16), b.astype(BF16), (((1,), (1,)), ((), ())),
                           preferred_element_type=F32)


def _dot_tn(a, b):
    return lax.dot_general(a.astype(BF16), b.astype(BF16), (((0,), (0,)), ((), ())),
                           preferred_element_type=F32)


def _split3(x):
    x1 = x.astype(BF16).astype(F32)
    r = x - x1
    x2 = r.astype(BF16).astype(F32)
    x3 = (r - x2).astype(BF16).astype(F32)
    return x1, x2, x3


def _dot_m01(m01, x):
    return _dot(jnp.concatenate([m01] * 3, axis=1), jnp.concatenate(_split3(x), axis=0))


def _dot_x01(x, m01):
    return _dot(jnp.concatenate(_split3(x), axis=1), jnp.concatenate([m01] * 3, axis=0))


def _sigmoid(x):
    return 1.0 / (1.0 + jnp.exp(-x))


def _silu(x):
    return x * _sigmoid(x)


def _softplus(x):
    return jnp.maximum(x, 0.0) + jnp.log(1.0 + jnp.exp(-jnp.abs(x)))


def _log_sigmoid(x):
    return -_softplus(-x)


def _rms(x):
    return x * lax.rsqrt(jnp.mean(x * x, axis=-1, keepdims=True) + EPS)


def _tri(c):
    row = lax.broadcasted_iota(jnp.int32, (c, c), 0)
    col = lax.broadcasted_iota(jnp.int32, (c, c), 1)
    return col <= row


def _row_valid(c, valid):
    return lax.broadcasted_iota(jnp.int32, (c, 1), 0) < valid


def _col_valid(c, valid):
    return lax.broadcasted_iota(jnp.int32, (1, c), 1) < valid


NORM_ROWS = 256


def _inproj_kernel(x_ref, nw_ref, w_ref, ws_ref, bs_ref, o_ref, os_ref, xn_ref):
    @pl.when(pl.program_id(1) == 0)
    def _():
        ws = ws_ref[...]
        wh = ws.astype(BF16)
        wl = (ws - wh.astype(F32)).astype(BF16)
        wcat = jnp.concatenate([wh, wl], axis=1)
        nr = min(NORM_ROWS, x_ref.shape[0])
        for r in range(x_ref.shape[0] // nr):
            rows = pl.ds(r * nr, nr)
            xn = _rms(x_ref[rows, :]) * nw_ref[...]
            xh = xn.astype(BF16)
            xl = (xn - xh.astype(F32)).astype(BF16)
            xn_ref[rows, :] = xh
            p = jnp.dot(xh, wcat, preferred_element_type=F32)
            os_ref[rows, :] = (p[:, :LANES] + p[:, LANES:]
                               + jnp.dot(xl, wh, preferred_element_type=F32) + bs_ref[...])

    o_ref[...] = jnp.dot(xn_ref[...], w_ref[...], preferred_element_type=F32)


def _inproj(x, norm_w, w_main, w_small, b_small, *, tm, tn):
    t, d = x.shape
    n = w_main.shape[1]
    return pl.pallas_call(
        _inproj_kernel,
        out_shape=(jax.ShapeDtypeStruct((t, n), F32), jax.ShapeDtypeStruct((t, LANES), F32)),
        grid=(t // tm, n // tn),
        in_specs=[
            pl.BlockSpec((tm, d), lambda i, j: (i, 0)),
            pl.BlockSpec((1, d), lambda i, j: (0, 0)),
            pl.BlockSpec((d, tn), lambda i, j: (0, j)),
            pl.BlockSpec((d, LANES), lambda i, j: (0, 0)),
            pl.BlockSpec((1, LANES), lambda i, j: (0, 0)),
        ],
        out_specs=(pl.BlockSpec((tm, tn), lambda i, j: (i, j)),
                   pl.BlockSpec((tm, LANES), lambda i, j: (i, 0))),
        scratch_shapes=[pltpu.VMEM((tm, d), BF16)],
        compiler_params=_cparams(("parallel", "arbitrary")),
        name="inproj",
    )(x, norm_w, w_main, w_small, b_small)


def _outproj_kernel(a_ref, b_ref, wa_ref, wb_ref, x_ref, o_ref):
    o_ref[...] = (x_ref[...]
                  + jnp.dot(a_ref[...], wa_ref[...], preferred_element_type=F32)
                  + jnp.dot(b_ref[...], wb_ref[...], preferred_element_type=F32))


def _outproj(a, b, a_blk, b_blk, w, x, *, tm, tn):
    t, d = x.shape
    half = D_INNER // 2
    return pl.pallas_call(
        _outproj_kernel,
        out_shape=jax.ShapeDtypeStruct((t, d), F32),
        grid=(t // tm, d // tn),
        in_specs=[
            pl.BlockSpec((tm, half), lambda i, j: (i, a_blk)),
            pl.BlockSpec((tm, half), lambda i, j: (i, b_blk)),
            pl.BlockSpec((half, tn), lambda i, j: (0, j)),
            pl.BlockSpec((half, tn), lambda i, j: (1, j)),
            pl.BlockSpec((tm, tn), lambda i, j: (i, j)),
        ],
        out_specs=pl.BlockSpec((tm, tn), lambda i, j: (i, j)),
        compiler_params=_cparams(("parallel", "arbitrary")),
        name="outproj",
    )(a, b, w, w, x)


def _final_norm_kernel(x_ref, w_ref, o_ref):
    o_ref[...] = _rms(x_ref[...]) * w_ref[...]


def _final_norm(x, w, *, tm):
    t, d = x.shape
    return pl.pallas_call(
        _final_norm_kernel,
        out_shape=jax.ShapeDtypeStruct((t, d), F32),
        grid=(t // tm,),
        in_specs=[pl.BlockSpec((tm, d), lambda i: (i, 0)),
                  pl.BlockSpec((1, d), lambda i: (0, 0))],
        out_specs=pl.BlockSpec((tm, d), lambda i: (i, 0)),
        compiler_params=_cparams(("parallel",)),
        name="final_norm",
    )(x, w)


def _causal_conv(buf_ref, x, w_ref, b_ref, c):
    buf_ref[pl.ds(SUBLANES, c), :] = x
    out = b_ref[...] + buf_ref[pl.ds(SUBLANES - HIST, c), :] * w_ref[0:1, :]
    for t in range(1, CONV):
        out = out + buf_ref[pl.ds(SUBLANES - HIST + t, c), :] * w_ref[t:t + 1, :]
    buf_ref[pl.ds(SUBLANES - HIST, HIST), :] = buf_ref[pl.ds(SUBLANES + c - HIST, HIST), :]
    return out


def _init_hist(buf_ref, hist):
    buf_ref[pl.ds(SUBLANES - HIST, HIST), :] = hist


def _gla_kernel(*refs, c, c_true, valid, has_init, has_prev):
    refs = list(refs)
    q_ref, k_ref, v_ref, z_ref, sm_ref, wa_ref, ba_ref, gn_ref = refs[:8]
    refs = refs[8:]
    s0_ref = refs.pop(0) if has_init else None
    if has_prev:
        refs.pop(0)
    o_ref, s_ref = refs

    @pl.when(pl.program_id(1) == 0)
    def _():
        if has_init:
            s_ref[0, 0] = s0_ref[0, 0]
        else:
            s_ref[0, 0] = jnp.zeros((HEADS, DK, DV), F32)

    ag = sm_ref[0][:, :GLA_LOWRANK]
    a1 = ag.astype(BF16).astype(F32)
    a2 = (ag - a1).astype(BF16).astype(F32)
    gpre = _dot(jnp.concatenate([a1, a1, a2], axis=1), wa_ref[...]) + ba_ref[...]
    g = _log_sigmoid(gpre) * (1.0 / GLA_GATE_TEMP)
    rv = _row_valid(c, valid) if valid < c else None
    if rv is not None:
        g = jnp.where(rv, g, 0.0)
    tri = _tri(c)
    g3 = jnp.concatenate(_split3(g), axis=0).astype(BF16)
    tri3 = jnp.concatenate([tri.astype(F32)] * 3, axis=1).astype(BF16)
    b = jnp.dot(tri3, g3, preferred_element_type=F32)
    dcol = jnp.exp(lax.dot_general(g3, jnp.ones((3 * c, LANES), BF16), (((0,), (0,)), ((), ())),
                                   preferred_element_type=F32))

    for h in range(HEADS):
        ks = slice(h * DK, (h + 1) * DK)
        vs = slice(h * DV, (h + 1) * DV)
        q = q_ref[0, :, ks] * (DK ** -0.5)
        k = k_ref[0, :, ks]
        if rv is not None:
            k = jnp.where(rv, k, 0.0)
        v = v_ref[0, :, vs]
        bh = b[:, ks]
        mid = bh[c_true // 2:c_true // 2 + 1, :]
        b_last = bh[c - 1:c, :]
        s = s_ref[0, 0, h]
        o = _dot(q * jnp.exp(bh), s)
        att = _dot_nt(q * jnp.exp(bh - mid), k * jnp.exp(mid - bh))
        o = o + _dot(jnp.where(tri, att, 0.0), v)
        dc = dcol[ks, :]
        s_ref[0, 0, h] = (s * jnp.concatenate([dc] * (DV // LANES), axis=1)
                          + _dot_tn(k * jnp.exp(b_last - bh), v))
        o_ref[0, :, vs] = (_rms(o) * gn_ref[:, vs] * _silu(z_ref[0, :, vs])).astype(BF16)


def _gla(main, small, wa3, ba, gnorm, s0, prev, *, layer, n_layers, c, c_true, valid):
    bsz, L, _ = main.shape
    nc = L // c
    has_init = s0 is not None
    has_prev = prev is not None
    kern = functools.partial(_gla_kernel, c=c, c_true=c_true, valid=valid,
                             has_init=has_init, has_prev=has_prev)
    in_specs = [
        pl.BlockSpec((1, c, QK), lambda b, n: (b, n, 0)),
        pl.BlockSpec((1, c, QK), lambda b, n: (b, n, 1)),
        pl.BlockSpec((1, c, WIDTH), lambda b, n: (b, n, 2 * QK // WIDTH)),
        pl.BlockSpec((1, c, WIDTH), lambda b, n: (b, n, 2 * QK // WIDTH + 1)),
        pl.BlockSpec((1, c, LANES), lambda b, n: (b, n, 0)),
        pl.BlockSpec((3 * GLA_LOWRANK, QK), lambda b, n: (0, 0)),
        pl.BlockSpec((1, QK), lambda b, n: (0, 0)),
        pl.BlockSpec((1, WIDTH), lambda b, n: (0, 0)),
    ]
    args = [main, main, main, main, small, wa3, ba, gnorm]
    if has_init:
        in_specs.append(pl.BlockSpec((1, 1, HEADS, DK, DV), lambda b, n: (layer, b, 0, 0, 0)))
        args.append(s0)
    aliases = {}
    if has_prev:
        aliases = {len(args): 1}
        in_specs.append(pl.BlockSpec(memory_space=pl.ANY))
        args.append(prev)
    return pl.pallas_call(
        kern,
        out_shape=(jax.ShapeDtypeStruct((bsz, L, WIDTH), BF16),
                   jax.ShapeDtypeStruct((n_layers, bsz, HEADS, DK, DV), F32)),
        grid=(bsz, nc),
        in_specs=in_specs,
        out_specs=(pl.BlockSpec((1, c, WIDTH), lambda b, n: (b, n, 0)),
                   pl.BlockSpec((1, 1, HEADS, DK, DV), lambda b, n: (layer, b, 0, 0, 0))),
        input_output_aliases=aliases,
        compiler_params=_cparams(("parallel", "arbitrary")),
        name="gla_scan",
    )(*args)


def _mlstm_kernel(*refs, c, valid, has_init, n_prev):
    refs = list(refs)
    qk_ref, v_ref, og_ref, z_ref, sm_ref, gt_ref, cw_ref, cb_ref, mn_ref = refs[:9]
    refs = refs[9:]
    if has_init:
        c0_ref, n0_ref, m0_ref, h0_ref = refs[:4]
        refs = refs[4:]
    refs = refs[n_prev:]
    o_ref, c_ref, n_ref, m_ref, buf = refs

    @pl.when(pl.program_id(1) == 0)
    def _():
        if has_init:
            c_ref[0, 0] = c0_ref[0, 0]
            n_ref[0, 0] = n0_ref[0, 0]
            m_ref[0, 0] = jnp.broadcast_to(m0_ref[0, 0], (HEADS, 1, LANES))
            _init_hist(buf, h0_ref[0, 0])
        else:
            c_ref[0, 0] = jnp.zeros((HEADS, DK, DV), F32)
            n_ref[0, 0] = jnp.zeros((HEADS, 1, DK), F32)
            m_ref[0, 0] = jnp.zeros((HEADS, 1, LANES), F32)
            _init_hist(buf, jnp.zeros((HIST, 2 * QK), F32))

    qk = _silu(_causal_conv(buf, qk_ref[0], cw_ref, cb_ref, c))
    sm = sm_ref[0]
    gt = gt_ref[0, 0]
    rv = _row_valid(c, valid) if valid < c else None
    cv = _col_valid(c, valid) if valid < c else None
    tri = _tri(c)
    row = lax.broadcasted_iota(jnp.int32, (c, c), 0)
    col = lax.broadcasted_iota(jnp.int32, (c, c), 1)
    triu = row <= col

    for h in range(HEADS):
        vs = slice(h * DV, (h + 1) * DV)
        q = qk[:, h * DK:(h + 1) * DK]
        k = qk[:, QK + h * DK:QK + (h + 1) * DK] * (DK ** -0.5)
        v = v_ref[0, :, vs]
        ig_c = sm[:, SM_I + h:SM_I + h + 1]
        lf_c = _log_sigmoid(sm[:, SM_F + h:SM_F + h + 1])
        ig_r = gt[h:h + 1, :]
        lf_r = _log_sigmoid(gt[HEADS + h:HEADS + h + 1, :])
        if rv is not None:
            lf_c = jnp.where(rv, lf_c, 0.0)
            lf_r = jnp.where(cv, lf_r, 0.0)
            ig_c = jnp.where(rv, ig_c, NEG)
            ig_r = jnp.where(cv, ig_r, NEG)
        fcum_c = jnp.sum(jnp.where(tri, lf_r, 0.0), axis=1, keepdims=True)
        fcum_r = jnp.sum(jnp.where(triu, lf_c, 0.0), axis=0, keepdims=True)
        m_prev = m_ref[0, 0, h][:, 0:1]
        dlog = jnp.where(tri, fcum_c - fcum_r + ig_r, NEG)
        inter = fcum_c + m_prev
        m_i = jnp.maximum(inter, jnp.max(dlog, axis=1, keepdims=True))
        w_inter = jnp.exp(inter - m_i)
        qkm = _dot_nt(q, k) * jnp.exp(dlog - m_i)
        cm = c_ref[0, 0, h]
        nm = n_ref[0, 0, h]
        num = _dot(qkm, v) + w_inter * _dot(q, cm)
        den = (jnp.sum(qkm, axis=1, keepdims=True)
               + w_inter * jnp.sum(q * nm, axis=1, keepdims=True))
        den = jnp.maximum(jnp.abs(den), jnp.exp(-m_i))
        hh = num / den
        m_new = m_i[c - 1:c, :]
        f_last = fcum_c[c - 1:c, :]
        w_j = jnp.exp(f_last - fcum_c + ig_c - m_new)
        decay = jnp.exp(f_last + m_prev - m_new)
        kw = w_j * k
        c_ref[0, 0, h] = decay * cm + _dot_tn(kw, v)
        n_ref[0, 0, h] = decay * nm + jnp.sum(kw, axis=0, keepdims=True)
        m_ref[0, 0, h] = jnp.broadcast_to(m_new, (1, LANES))
        hm = _sigmoid(og_ref[0, :, vs]) * hh
        o_ref[0, :, vs] = (_rms(hm) * mn_ref[:, vs] * _silu(z_ref[0, :, vs])).astype(BF16)


def _mlstm(main, small, gates_t, cw, cb, mnorm, init, prev, *, layer, n_layers, c, valid):
    bsz, L, _ = main.shape
    nc = L // c
    has_init = init is not None
    n_prev = 0 if prev is None else len(prev)
    kern = functools.partial(_mlstm_kernel, c=c, valid=valid, has_init=has_init, n_prev=n_prev)
    base = (2 * QK + 2 * WIDTH) // WIDTH
    in_specs = [
        pl.BlockSpec((1, c, 2 * QK), lambda b, n: (b, n, base)),
        pl.BlockSpec((1, c, WIDTH), lambda b, n: (b, n, base + 1)),
        pl.BlockSpec((1, c, WIDTH), lambda b, n: (b, n, base + 2)),
        pl.BlockSpec((1, c, WIDTH), lambda b, n: (b, n, base + 3)),
        pl.BlockSpec((1, c, LANES), lambda b, n: (b, n, 0)),
        pl.BlockSpec((1, 1, 2 * HEADS, c), lambda b, n: (b, n, 0, 0)),
        pl.BlockSpec((CONV, 2 * QK), lambda b, n: (0, 0)),
        pl.BlockSpec((1, 2 * QK), lambda b, n: (0, 0)),
        pl.BlockSpec((1, WIDTH), lambda b, n: (0, 0)),
    ]
    args = [main, main, main, main, small, gates_t, cw, cb, mnorm]
    if has_init:
        c0, n0, m0, conv0 = init
        in_specs += [
            pl.BlockSpec((1, 1, HEADS, DK, DV), lambda b, n: (layer, b, 0, 0, 0)),
            pl.BlockSpec((1, 1, HEADS, 1, DK), lambda b, n: (layer, b, 0, 0, 0)),
            pl.BlockSpec((1, 1, HEADS, 1, 1), lambda b, n: (layer, b, 0, 0, 0)),
            pl.BlockSpec((1, 1, HIST, 2 * QK), lambda b, n: (layer, b, 0, 0)),
        ]
        args += [c0, n0, m0, conv0]
    aliases = {}
    if n_prev:
        for i, p in enumerate(prev):
            aliases[len(args)] = 1 + i
            in_specs.append(pl.BlockSpec(memory_space=pl.ANY))
            args.append(p)
    return pl.pallas_call(
        kern,
        out_shape=(jax.ShapeDtypeStruct((bsz, L, WIDTH), BF16),
                   jax.ShapeDtypeStruct((n_layers, bsz, HEADS, DK, DV), F32),
                   jax.ShapeDtypeStruct((n_layers, bsz, HEADS, 1, DK), F32),
                   jax.ShapeDtypeStruct((n_layers, bsz, HEADS, 1, LANES), F32)),
        grid=(bsz, nc),
        in_specs=in_specs,
        out_specs=(pl.BlockSpec((1, c, WIDTH), lambda b, n: (b, n, 0)),
                   pl.BlockSpec((1, 1, HEADS, DK, DV), lambda b, n: (layer, b, 0, 0, 0)),
                   pl.BlockSpec((1, 1, HEADS, 1, DK), lambda b, n: (layer, b, 0, 0, 0)),
                   pl.BlockSpec((1, 1, HEADS, 1, LANES), lambda b, n: (layer, b, 0, 0, 0))),
        scratch_shapes=[pltpu.VMEM((SUBLANES + c, 2 * QK), F32)],
        input_output_aliases=aliases,
        compiler_params=_cparams(("parallel", "arbitrary")),
        name="mlstm_scan",
    )(*args)


def _pad_rows(a, rows):
    if a.shape[0] == rows:
        return a
    return jnp.concatenate([a, jnp.zeros((rows - a.shape[0], a.shape[1]), a.dtype)], axis=0)


def _ssd_kernel(*refs, c, valid, has_init, has_prev, transposed, last):
    refs = list(refs)
    (z_ref, xp_ref, bcp_ref, sm_ref, cwx_ref, cwbc_ref, cbx_ref, cbbc_ref,
     alog_ref, d_ref, nw_ref) = refs[:11]
    refs = refs[11:]
    if has_init:
        s0_ref, hx_ref, hbc_ref = refs[:3]
        refs = refs[3:]
    if has_prev:
        refs = refs[1:]
    if transposed:
        o_ref, s_ref, xbuf, bcbuf, st_ref = refs
    else:
        o_ref, s_ref, xbuf, bcbuf = refs

    @pl.when(pl.program_id(1) == 0)
    def _():
        if has_init:
            _init_hist(xbuf, hx_ref[0, 0])
            _init_hist(bcbuf, hbc_ref[0, 0])
            if transposed:
                for g in range(SSD_G):
                    st_ref[g] = jnp.transpose(s0_ref[0, 0, g])
            else:
                s_ref[0, 0] = s0_ref[0, 0]
        else:
            _init_hist(xbuf, jnp.zeros((HIST, D_INNER), F32))
            _init_hist(bcbuf, jnp.zeros((HIST, SSD_BC), F32))
            if transposed:
                st_ref[...] = jnp.zeros((SSD_G, SSD_N, SSD_GW), F32)
            else:
                s_ref[0, 0] = jnp.zeros((SSD_G, SSD_GW, SSD_N), F32)

    xall = _silu(_causal_conv(xbuf, xp_ref[0], cwx_ref, cbx_ref, c))
    bcall = _silu(_causal_conv(bcbuf, bcp_ref[0], cwbc_ref, cbbc_ref, c))
    dt_all = _softplus(sm_ref[0][:, :SSD_HEADS])
    if valid < c:
        dt_all = jnp.where(_row_valid(c, valid), dt_all, 0.0)
    a_all = -jnp.exp(alog_ref[...])
    tri = _tri(c).astype(F32)
    acum_all = _dot_m01(tri, dt_all * a_all)

    e = (lax.shift_right_logical(lax.broadcasted_iota(jnp.int32, (SSD_HPG, SSD_GW), 1), SLOT_LOG2)
         == lax.broadcasted_iota(jnp.int32, (SSD_HPG, SSD_GW), 0)).astype(F32)
    rowi = lax.broadcasted_iota(jnp.int32, (c, SSD_GW), 0)
    jpos = jnp.bitwise_and(lax.broadcasted_iota(jnp.int32, (c, SSD_GW), 1), SLOT - 1)
    causal = jpos <= rowi
    diag = jpos == rowi
    sel = (lax.shift_right_logical(lax.broadcasted_iota(jnp.int32, (2 * SLOT, 2 * SSD_P), 0), SLOT_LOG2)
           == lax.shift_right_logical(lax.broadcasted_iota(jnp.int32, (2 * SLOT, 2 * SSD_P), 1), SLOT_LOG2))

    for g in range(SSD_G):
        gs = slice(g * SSD_GW, (g + 1) * SSD_GW)
        hs = slice(g * SSD_HPG, (g + 1) * SSD_HPG)
        x = xall[:, gs]
        bm = bcall[:, g * SSD_N:(g + 1) * SSD_N]
        cm = bcall[:, SSD_G * SSD_N + g * SSD_N:SSD_G * SSD_N + (g + 1) * SSD_N]
        ex = _dot_x01(jnp.concatenate([acum_all[:, hs], dt_all[:, hs]], axis=0), e)
        a_exp = ex[:c]
        dt_exp = ex[c:]
        a_row = jnp.sum(jnp.where(diag, a_exp, 0.0), axis=0, keepdims=True)
        dec = jnp.exp(jnp.where(causal, a_exp - a_row, NEG))
        bm_t = jnp.concatenate([_pad_rows(bm, SLOT)] * SSD_HPG, axis=0)
        m = (_dot_nt(cm, bm_t) * dec).astype(BF16)
        xdt = _pad_rows(x * dt_exp, SLOT).astype(BF16)
        ys = []
        for pr in range(SSD_HPG // 2):
            ps = slice(pr * 2 * SSD_P, (pr + 1) * 2 * SSD_P)
            xp = xdt[:, ps]
            xblk = jnp.where(sel, jnp.concatenate([xp, xp], axis=0), 0.0)
            ys.append(jnp.dot(m[:, ps], xblk, preferred_element_type=F32))
        y = jnp.concatenate(ys, axis=1)
        a_last = a_exp[c - 1:c, :]
        wx = x * (jnp.exp(a_last - a_exp) * dt_exp)
        if transposed:
            st = st_ref[g]
            y = y + _dot(cm, st) * jnp.exp(a_exp)
            st_ref[g] = st * jnp.exp(a_last) + _dot_tn(bm, wx)
        else:
            s = s_ref[0, 0, g]
            y = y + _dot_nt(cm, s) * jnp.exp(a_exp)
            upd = _dot_tn(wx, bm)
            sdec = jnp.exp(acum_all[c - 1:c, hs])
            for hh in range(SSD_HPG):
                rs = slice(hh * SSD_P, (hh + 1) * SSD_P)
                s_ref[0, 0, g, rs, :] = s[rs, :] * sdec[:, hh:hh + 1] + upd[rs, :]
        y = y + d_ref[:, gs] * x
        y = y * _silu(z_ref[0, :, gs])
        o_ref[0, :, gs] = (_rms(y) * nw_ref[:, gs]).astype(BF16)

    if transposed:
        @pl.when(pl.program_id(1) == last)
        def _():
            for g in range(SSD_G):
                s_ref[0, 0, g] = jnp.transpose(st_ref[g])


def _ssd(main, small, cw, cb, alog, d_exp, normw, init, prev, *, layer, n_layers, c, valid,
         transposed):
    bsz, L, _ = main.shape
    nc = L // c
    has_init = init is not None
    has_prev = prev is not None
    kern = functools.partial(_ssd_kernel, c=c, valid=valid, has_init=has_init, has_prev=has_prev,
                             transposed=transposed, last=nc - 1)
    bc_blk = 2 * D_INNER // SSD_BC
    in_specs = [
        pl.BlockSpec((1, c, D_INNER), lambda b, n: (b, n, 0)),
        pl.BlockSpec((1, c, D_INNER), lambda b, n: (b, n, 1)),
        pl.BlockSpec((1, c, SSD_BC), lambda b, n: (b, n, bc_blk)),
        pl.BlockSpec((1, c, LANES), lambda b, n: (b, n, 0)),
        pl.BlockSpec((CONV, D_INNER), lambda b, n: (0, 0)),
        pl.BlockSpec((CONV, SSD_BC), lambda b, n: (0, D_INNER // SSD_BC)),
        pl.BlockSpec((1, D_INNER), lambda b, n: (0, 0)),
        pl.BlockSpec((1, SSD_BC), lambda b, n: (0, D_INNER // SSD_BC)),
        pl.BlockSpec((1, SSD_HEADS), lambda b, n: (0, 0)),
        pl.BlockSpec((1, D_INNER), lambda b, n: (0, 0)),
        pl.BlockSpec((1, D_INNER), lambda b, n: (0, 0)),
    ]
    args = [main, main, main, small, cw, cw, cb, cb, alog, d_exp, normw]
    if has_init:
        s0, conv0 = init
        in_specs += [
            pl.BlockSpec((1, 1, SSD_G, SSD_GW, SSD_N), lambda b, n: (layer, b, 0, 0, 0)),
            pl.BlockSpec((1, 1, HIST, D_INNER), lambda b, n: (layer, b, 0, 0)),
            pl.BlockSpec((1, 1, HIST, SSD_BC), lambda b, n: (layer, b, 0, D_INNER // SSD_BC)),
        ]
        args += [s0, conv0, conv0]
    aliases = {}
    if has_prev:
        aliases = {len(args): 1}
        in_specs.append(pl.BlockSpec(memory_space=pl.ANY))
        args.append(prev)
    scratch = [pltpu.VMEM((SUBLANES + c, D_INNER), F32), pltpu.VMEM((SUBLANES + c, SSD_BC), F32)]
    if transposed:
        scratch.append(pltpu.VMEM((SSD_G, SSD_N, SSD_GW), F32))
    return pl.pallas_call(
        kern,
        out_shape=(jax.ShapeDtypeStruct((bsz, L, D_INNER), BF16),
                   jax.ShapeDtypeStruct((n_layers, bsz, SSD_G, SSD_GW, SSD_N), F32)),
        grid=(bsz, nc),
        in_specs=in_specs,
        out_specs=(pl.BlockSpec((1, c, D_INNER), lambda b, n: (b, n, 0)),
                   pl.BlockSpec((1, 1, SSD_G, SSD_GW, SSD_N), lambda b, n: (layer, b, 0, 0, 0))),
        scratch_shapes=scratch,
        input_output_aliases=aliases,
        compiler_params=_cparams(("parallel", "arbitrary")),
        name="ssd_scan",
    )(*args)


def _prep_params(p):
    n_ab = p['w_in_ab'].shape[0]
    n_c = p['w_in_c'].shape[0]
    o_ag = 2 * QK + 2 * WIDTH
    o_m = o_ag + GLA_LOWRANK
    o_if = o_m + 2 * QK + 3 * WIDTH
    ab, cc = [], []
    for i in range(n_ab):
        w = p['w_in_ab'][i]
        w_main = jnp.concatenate([w[:, :o_ag], w[:, o_m:o_if]], axis=1).astype(BF16)
        w_small = jnp.zeros((D_MODEL, LANES), F32)
        w_small = w_small.at[:, :GLA_LOWRANK].set(w[:, o_ag:o_m])
        w_small = w_small.at[:, SM_I:SM_I + 2 * HEADS].set(w[:, o_if:])
        b_small = jnp.zeros((1, LANES), F32)
        b_small = b_small.at[0, SM_I:SM_I + HEADS].set(p['mlstm_b_i'][i])
        b_small = b_small.at[0, SM_F:SM_F + HEADS].set(p['mlstm_b_f'][i])
        wa = p['gla_w_a2'][i]
        wa_hi = wa.astype(BF16)
        wa_lo = (wa - wa_hi.astype(F32)).astype(BF16)
        ab.append(dict(
            norm=p['norm_ab'][i][None], w_main=w_main, w_small=w_small, b_small=b_small,
            wa3=jnp.concatenate([wa_hi, wa_lo, wa_hi], axis=0),
            ba=p['gla_b_a'][i][None], gnorm=p['gla_norm'][i][None],
            cw=p['mlstm_conv_w'][i], cb=p['mlstm_conv_b'][i][None],
            mnorm=p['mlstm_norm'][i][None], w_out=p['w_out_ab'][i].astype(BF16)))
    o_dt = D_INNER + SSD_CONV_DIM
    for i in range(n_c):
        w = p['w_in_c'][i]
        w_small = jnp.zeros((D_MODEL, LANES), F32).at[:, :SSD_HEADS].set(w[:, o_dt:])
        b_small = jnp.zeros((1, LANES), F32).at[0, :SSD_HEADS].set(p['ssd_dt_bias'][i])
        cc.append(dict(
            norm=p['norm_c'][i][None], w_main=w[:, :o_dt].astype(BF16), w_small=w_small,
            b_small=b_small, cw=p['ssd_conv_w'][i], cb=p['ssd_conv_b'][i][None],
            alog=p['ssd_a_log'][i][None],
            d_exp=jnp.repeat(p['ssd_d'][i], SSD_P)[None],
            normw=p['ssd_norm'][i][None], w_out=p['w_out_c'][i].astype(BF16)))
    return ab, cc


def _trunk(x, states, ab, cc, final_w, *, c, c_true, valid, tm, transposed):
    bsz, L, _ = x.shape
    t = bsz * L
    nc = L // c
    n_ab, n_c = len(ab), len(cc)
    xt = x.reshape(t, D_MODEL)
    gla_o = mc_o = mn_o = mm_o = ssm_o = None
    mconv, sconv = [], []
    row0 = (nc - 1) * c + valid - HIST
    for layer in range(n_ab + n_c):
        i = layer // 2
        if layer % 2 == 0:
            P = ab[i]
            main, small = _inproj(xt, P['norm'], P['w_main'], P['w_small'], P['b_small'],
                                  tm=tm, tn=1024)
            main = main.reshape(bsz, L, -1)
            small = small.reshape(bsz, L, LANES)
            gates_t = jnp.swapaxes(
                small[:, :, SM_I:SM_I + 2 * HEADS].reshape(bsz, nc, c, 2 * HEADS), 2, 3)
            s_gla = None if states is None else states['gla']
            og, gla_o = _gla(main, small, P['wa3'], P['ba'], P['gnorm'], s_gla, gla_o,
                             layer=i, n_layers=n_ab, c=c, c_true=c_true, valid=valid)
            init = None if states is None else (states['mc'], states['mn'], states['mm'],
                                                states['mconv'])
            prev = None if mc_o is None else (mc_o, mn_o, mm_o)
            om, mc_o, mn_o, mm_o = _mlstm(main, small, gates_t, P['cw'], P['cb'], P['mnorm'],
                                          init, prev, layer=i, n_layers=n_ab, c=c, valid=valid)
            mconv.append(main[:, row0:row0 + HIST, 2 * QK + 2 * WIDTH:2 * QK + 2 * WIDTH + 2 * QK])
            xt = _outproj(og.reshape(t, WIDTH), om.reshape(t, WIDTH), 0, 0, P['w_out'], xt,
                          tm=min(tm, 512), tn=512)
        else:
            P = cc[i]
            main, small = _inproj(xt, P['norm'], P['w_main'], P['w_small'], P['b_small'],
                                  tm=tm, tn=1024)
            main = main.reshape(bsz, L, -1)
            small = small.reshape(bsz, L, LANES)
            init = None if states is None else (states['ssm'], states['sconv'])
            y, ssm_o = _ssd(main, small, P['cw'], P['cb'], P['alog'], P['d_exp'], P['normw'],
                            init, ssm_o, layer=i, n_layers=n_c, c=c, valid=valid,
                            transposed=transposed)
            sconv.append(main[:, row0:row0 + HIST, D_INNER:D_INNER + SSD_CONV_DIM])
            y2 = y.reshape(t, D_INNER)
            xt = _outproj(y2, y2, 0, 1, P['w_out'], xt, tm=min(tm, 512), tn=512)
    y = _final_norm(xt, final_w, tm=min(tm, 512)).reshape(bsz, L, D_MODEL)
    return (y, gla_o, mc_o, mn_o[:, :, :, 0, :], mm_o[:, :, :, 0, 0],
            jnp.stack(mconv), ssm_o.reshape(n_c, bsz, SSD_HEADS, SSD_P, SSD_N), jnp.stack(sconv))


def kernel(x_prompt, x_sample, state_gla, state_mlstm_c, state_mlstm_n, state_mlstm_m, state_mlstm_conv, state_ssm, state_ssm_conv, norm_ab, w_in_ab, gla_w_a2, gla_b_a, gla_norm, mlstm_conv_w, mlstm_conv_b, mlstm_b_i, mlstm_b_f, mlstm_norm, w_out_ab, norm_c, w_in_c, ssd_conv_w, ssd_conv_b, ssd_dt_bias, ssd_a_log, ssd_d, ssd_norm, w_out_c, final_norm):
    p = dict(norm_ab=norm_ab, w_in_ab=w_in_ab, gla_w_a2=gla_w_a2, gla_b_a=gla_b_a, gla_norm=gla_norm,
             mlstm_conv_w=mlstm_conv_w, mlstm_conv_b=mlstm_conv_b, mlstm_b_i=mlstm_b_i,
             mlstm_b_f=mlstm_b_f, mlstm_norm=mlstm_norm, w_out_ab=w_out_ab, norm_c=norm_c,
             w_in_c=w_in_c, ssd_conv_w=ssd_conv_w, ssd_conv_b=ssd_conv_b, ssd_dt_bias=ssd_dt_bias,
             ssd_a_log=ssd_a_log, ssd_d=ssd_d, ssd_norm=ssd_norm, w_out_c=w_out_c)
    ab, cc = _prep_params(p)
    fw = final_norm[None]
    n_ab, dec_b = state_gla.shape[0], state_gla.shape[1]
    n_c = state_ssm.shape[0]
    dec_l = x_sample.shape[1]

    yp, gla_p, mc_p, mn_p, mm_p, mcv_p, ssm_p, scv_p = _trunk(
        x_prompt, None, ab, cc, fw, c=CHUNK, c_true=CHUNK, valid=CHUNK, tm=1024, transposed=True)

    xs = jnp.pad(x_sample, ((0, 0), (0, SUBLANES - dec_l), (0, 0)))
    states = dict(
        gla=state_gla, mc=state_mlstm_c,
        mn=state_mlstm_n.reshape(n_ab, dec_b, HEADS, 1, DK),
        mm=state_mlstm_m.reshape(n_ab, dec_b, HEADS, 1, 1),
        mconv=state_mlstm_conv,
        ssm=state_ssm.reshape(n_c, dec_b, SSD_G, SSD_GW, SSD_N),
        sconv=state_ssm_conv)
    ys, gla_s, mc_s, mn_s, mm_s, mcv_s, ssm_s, scv_s = _trunk(
        xs, states, ab, cc, fw, c=SUBLANES, c_true=dec_l, valid=dec_l, tm=1024, transposed=False)
    ys = ys[:, :dec_l]
    return (yp, ys, gla_p, gla_s, mc_p, mc_s, mn_p, mn_s, mm_p, mm_s,
            mcv_p, mcv_s, ssm_p, ssm_s, scv_p, scv_s)
```

```python
import functools

import jax
import jax.numpy as jnp
from jax import lax
from jax.experimental import pallas as pl
from jax.experimental.pallas import tpu as pltpu

F32 = jnp.float32
BF16 = jnp.bfloat16

D_MODEL = 2048
D_INNER = 4096
CHUNK = 64
EPS = 1e-6
NEG = -1e30
HEADS = 4
DK = 256
DV = 512
QK = HEADS * DK
WIDTH = HEADS * DV
GLA_LOWRANK = 16
GLA_GATE_TEMP = 16.0
CONV = 4
HIST = CONV - 1
SM_I = 16
SM_F = 20
SSD_HEADS = 64
SSD_P = 64
SSD_G = 8
SSD_HPG = 8
SSD_N = 128
SSD_GW = SSD_HPG * SSD_P
SSD_BC = 2 * SSD_G * SSD_N
SSD_CONV_DIM = D_INNER + SSD_BC
LANES = 128
SUBLANES = 8
SLOT = 64
SLOT_LOG2 = 6

VMEM_LIMIT = 56 * 1024 * 1024


def _cparams(sem):
    return pltpu.CompilerParams(dimension_semantics=sem, vmem_limit_bytes=VMEM_LIMIT)


def _dot(a, b):
    return jnp.dot(a.astype(BF16), b.astype(BF16), preferred_element_type=F32)


def _dot_nt(a, b):
    return lax.dot_general(a.astype(BF16), b.astype(BF16), (((1,), (1,)), ((), ())),
                           preferred_element_type=F32)


def _dot_tn(a, b):
    return lax.dot_general(a.astype(BF16), b.astype(BF16), (((0,), (0,)), ((), ())),
                           preferred_element_type=F32)


def _split3(x):
    x1 = x.astype(BF16).astype(F32)
    r = x - x1
    x2 = r.astype(BF16).astype(F32)
    x3 = (r - x2).astype(BF16).astype(F32)
    return x1, x2, x3


def _dot_m01(m01, x):
    return _dot(jnp.concatenate([m01] * 3, axis=1), jnp.concatenate(_split3(x), axis=0))


def _dot_x01(x, m01):
    return _dot(jnp.concatenate(_split3(x), axis=1), jnp.concatenate([m01] * 3, axis=0))


def _sigmoid(x):
    return 1.0 / (1.0 + jnp.exp(-x))


def _silu(x):
    return x * _sigmoid(x)


def _softplus(x):
    return jnp.maximum(x, 0.0) + jnp.log(1.0 + jnp.exp(-jnp.abs(x)))


def _log_sigmoid(x):
    return -_softplus(-x)


def _rms(x):
    return x * lax.rsqrt(jnp.mean(x * x, axis=-1, keepdims=True) + EPS)


def _tri(c):
    row = lax.broadcasted_iota(jnp.int32, (c, c), 0)
    col = lax.broadcasted_iota(jnp.int32, (c, c), 1)
    return col <= row


def _row_valid(c, valid):
    return lax.broadcasted_iota(jnp.int32, (c, 1), 0) < valid


def _col_valid(c, valid):
    return lax.broadcasted_iota(jnp.int32, (1, c), 1) < valid


NORM_ROWS = 256


def _inproj_kernel(x_ref, nw_ref, w_ref, ws_ref, bs_ref, o_ref, os_ref, xn_ref):
    @pl.when(pl.program_id(1) == 0)
    def _():
        ws = ws_ref[...]
        wh = ws.astype(BF16)
        wl = (ws - wh.astype(F32)).astype(BF16)
        wcat = jnp.concatenate([wh, wl], axis=1)
        nr = min(NORM_ROWS, x_ref.shape[0])
        for r in range(x_ref.shape[0] // nr):
            rows = pl.ds(r * nr, nr)
            xn = _rms(x_ref[rows, :]) * nw_ref[...]
            xh = xn.astype(BF16)
            xl = (xn - xh.astype(F32)).astype(BF16)
            xn_ref[rows, :] = xh
            p = jnp.dot(xh, wcat, preferred_element_type=F32)
            os_ref[rows, :] = (p[:, :LANES] + p[:, LANES:]
                               + jnp.dot(xl, wh, preferred_element_type=F32) + bs_ref[...])

    o_ref[...] = jnp.dot(xn_ref[...], w_ref[...], preferred_element_type=F32)


def _inproj(x, norm_w, w_main, layer, w_small, b_small, *, tm, tn):
    t, d = x.shape
    n = w_main.shape[2]
    return pl.pallas_call(
        _inproj_kernel,
        out_shape=(jax.ShapeDtypeStruct((t, n), F32), jax.ShapeDtypeStruct((t, LANES), F32)),
        grid=(t // tm, n // tn),
        in_specs=[
            pl.BlockSpec((tm, d), lambda i, j: (i, 0)),
            pl.BlockSpec((1, d), lambda i, j: (0, 0)),
            pl.BlockSpec((None, d, tn), lambda i, j: (layer, 0, j)),
            pl.BlockSpec((d, LANES), lambda i, j: (0, 0)),
            pl.BlockSpec((1, LANES), lambda i, j: (0, 0)),
        ],
        out_specs=(pl.BlockSpec((tm, tn), lambda i, j: (i, j)),
                   pl.BlockSpec((tm, LANES), lambda i, j: (i, 0))),
        scratch_shapes=[pltpu.VMEM((tm, d), BF16)],
        compiler_params=_cparams(("parallel", "arbitrary")),
        name="inproj",
    )(x, norm_w, w_main, w_small, b_small)


CAST_COLS = 1024


def _cast_kernel(w_ref, o_ref, *, segs):
    for src, width, dst in segs:
        for off in range(0, width, CAST_COLS):
            n = min(CAST_COLS, width - off)
            o_ref[0, :, dst + off:dst + off + n] = w_ref[0, :, src + off:src + off + n].astype(BF16)


def _cast_weights(w, segs, *, tr):
    nl, rows, cols = w.shape
    n_out = sum(width for _, width, _ in segs)
    return pl.pallas_call(
        functools.partial(_cast_kernel, segs=segs),
        out_shape=jax.ShapeDtypeStruct((nl, rows, n_out), BF16),
        grid=(nl, rows // tr),
        in_specs=[pl.BlockSpec((1, tr, cols), lambda l, i: (l, i, 0))],
        out_specs=pl.BlockSpec((1, tr, n_out), lambda l, i: (l, i, 0)),
        compiler_params=_cparams(("parallel", "parallel")),
        name="cast_weights",
    )(w)


def _outproj_kernel(a_ref, b_ref, wa_ref, wb_ref, x_ref, o_ref):
    o_ref[...] = (x_ref[...]
                  + jnp.dot(a_ref[...], wa_ref[...], preferred_element_type=F32)
                  + jnp.dot(b_ref[...], wb_ref[...], preferred_element_type=F32))


def _outproj(a, b, a_blk, b_blk, w, layer, x, *, tm, tn):
    t, d = x.shape
    half = D_INNER // 2
    return pl.pallas_call(
        _outproj_kernel,
        out_shape=jax.ShapeDtypeStruct((t, d), F32),
        grid=(t // tm, d // tn),
        in_specs=[
            pl.BlockSpec((tm, half), lambda i, j: (i, a_blk)),
            pl.BlockSpec((tm, half), lambda i, j: (i, b_blk)),
            pl.BlockSpec((None, half, tn), lambda i, j: (layer, 0, j)),
            pl.BlockSpec((None, half, tn), lambda i, j: (layer, 1, j)),
            pl.BlockSpec((tm, tn), lambda i, j: (i, j)),
        ],
        out_specs=pl.BlockSpec((tm, tn), lambda i, j: (i, j)),
        compiler_params=_cparams(("parallel", "arbitrary")),
        name="outproj",
    )(a, b, w, w, x)


def _final_norm_kernel(x_ref, w_ref, o_ref):
    o_ref[...] = _rms(x_ref[...]) * w_ref[...]


def _final_norm(x, w, *, tm):
    t, d = x.shape
    return pl.pallas_call(
        _final_norm_kernel,
        out_shape=jax.ShapeDtypeStruct((t, d), F32),
        grid=(t // tm,),
        in_specs=[pl.BlockSpec((tm, d), lambda i: (i, 0)),
                  pl.BlockSpec((1, d), lambda i: (0, 0))],
        out_specs=pl.BlockSpec((tm, d), lambda i: (i, 0)),
        compiler_params=_cparams(("parallel",)),
        name="final_norm",
    )(x, w)


def _causal_conv(buf_ref, x, w_ref, b_ref, c):
    buf_ref[pl.ds(SUBLANES, c), :] = x
    xp = buf_ref[...]
    out = b_ref[...] + x * w_ref[CONV - 1:CONV, :]
    for s in range(1, CONV):
        out = out + pltpu.roll(xp, s, 0)[SUBLANES:, :] * w_ref[CONV - 1 - s:CONV - s, :]
    buf_ref[pl.ds(SUBLANES - HIST, HIST), :] = buf_ref[pl.ds(SUBLANES + c - HIST, HIST), :]
    return out


def _init_hist(buf_ref, hist):
    buf_ref[pl.ds(SUBLANES - HIST, HIST), :] = hist


def _gla_body(q_ref, k_ref, v_ref, z_ref, sm_ref, wa_ref, ba_ref, gn_ref, s0_ref, o_ref, s_ref,
              *, c, c_true, valid):
    @pl.when(pl.program_id(1) == 0)
    def _():
        if s0_ref is not None:
            s_ref[0, 0] = s0_ref[0, 0]
        else:
            s_ref[0, 0] = jnp.zeros((HEADS, DK, DV), F32)

    ag = sm_ref[0][:, :GLA_LOWRANK]
    a1 = ag.astype(BF16).astype(F32)
    a2 = (ag - a1).astype(BF16).astype(F32)
    gpre = _dot(jnp.concatenate([a1, a1, a2], axis=1), wa_ref[...]) + ba_ref[...]
    g = _log_sigmoid(gpre) * (1.0 / GLA_GATE_TEMP)
    rv = _row_valid(c, valid) if valid < c else None
    if rv is not None:
        g = jnp.where(rv, g, 0.0)
    tri = _tri(c)
    g3 = jnp.concatenate(_split3(g), axis=0).astype(BF16)
    tri3 = jnp.concatenate([tri.astype(F32)] * 3, axis=1).astype(BF16)
    b = jnp.dot(tri3, g3, preferred_element_type=F32)
    dcol = jnp.exp(lax.dot_general(g3, jnp.ones((3 * c, LANES), BF16), (((0,), (0,)), ((), ())),
                                   preferred_element_type=F32))

    for h in range(HEADS):
        ks = slice(h * DK, (h + 1) * DK)
        vs = slice(h * DV, (h + 1) * DV)
        q = q_ref[0, :, ks] * (DK ** -0.5)
        k = k_ref[0, :, ks]
        if rv is not None:
            k = jnp.where(rv, k, 0.0)
        v = v_ref[0, :, vs]
        bh = b[:, ks]
        mid = bh[c_true // 2:c_true // 2 + 1, :]
        b_last = bh[c - 1:c, :]
        s = s_ref[0, 0, h]
        o = _dot(q * jnp.exp(bh), s)
        att = _dot_nt(q * jnp.exp(bh - mid), k * jnp.exp(mid - bh))
        o = o + _dot(jnp.where(tri, att, 0.0), v)
        dc = dcol[ks, :]
        s_ref[0, 0, h] = (s * jnp.concatenate([dc] * (DV // LANES), axis=1)
                          + _dot_tn(k * jnp.exp(b_last - bh), v))
        o_ref[0, :, vs] = (_rms(o) * gn_ref[:, vs] * _silu(z_ref[0, :, vs])).astype(BF16)


def _mlstm_body(qk_ref, v_ref, og_ref, z_ref, sm_ref, gt_ref, cw_ref, cb_ref, mn_ref, init_refs,
                o_ref, c_ref, n_ref, m_ref, buf, *, c, valid):
    has_init = init_refs is not None
    if has_init:
        c0_ref, n0_ref, m0_ref, h0_ref = init_refs

    @pl.when(pl.program_id(1) == 0)
    def _():
        if has_init:
            c_ref[0, 0] = c0_ref[0, 0]
            n_ref[0, 0] = n0_ref[0, 0]
            m_ref[0, 0] = jnp.broadcast_to(m0_ref[0, 0], (HEADS, 1, LANES))
            _init_hist(buf, h0_ref[0, 0])
        else:
            c_ref[0, 0] = jnp.zeros((HEADS, DK, DV), F32)
            n_ref[0, 0] = jnp.zeros((HEADS, 1, DK), F32)
            m_ref[0, 0] = jnp.zeros((HEADS, 1, LANES), F32)
            _init_hist(buf, jnp.zeros((HIST, 2 * QK), F32))

    qk = _silu(_causal_conv(buf, qk_ref[0], cw_ref, cb_ref, c))
    sm = sm_ref[0]
    gt = gt_ref[0, 0]
    rv = _row_valid(c, valid) if valid < c else None
    cv = _col_valid(c, valid) if valid < c else None
    tri = _tri(c)
    row = lax.broadcasted_iota(jnp.int32, (c, c), 0)
    col = lax.broadcasted_iota(jnp.int32, (c, c), 1)
    triu = row <= col

    for h in range(HEADS):
        vs = slice(h * DV, (h + 1) * DV)
        q = qk[:, h * DK:(h + 1) * DK]
        k = qk[:, QK + h * DK:QK + (h + 1) * DK] * (DK ** -0.5)
        v = v_ref[0, :, vs]
        ig_c = sm[:, SM_I + h:SM_I + h + 1]
        lf_c = _log_sigmoid(sm[:, SM_F + h:SM_F + h + 1])
        ig_r = gt[h:h + 1, :]
        lf_r = _log_sigmoid(gt[HEADS + h:HEADS + h + 1, :])
        if rv is not None:
            lf_c = jnp.where(rv, lf_c, 0.0)
            lf_r = jnp.where(cv, lf_r, 0.0)
            ig_c = jnp.where(rv, ig_c, NEG)
            ig_r = jnp.where(cv, ig_r, NEG)
        fcum_c = jnp.sum(jnp.where(tri, lf_r, 0.0), axis=1, keepdims=True)
        fcum_r = jnp.sum(jnp.where(triu, lf_c, 0.0), axis=0, keepdims=True)
        m_prev = m_ref[0, 0, h][:, 0:1]
        dlog = jnp.where(tri, fcum_c - fcum_r + ig_r, NEG)
        inter = fcum_c + m_prev
        m_i = jnp.maximum(inter, jnp.max(dlog, axis=1, keepdims=True))
        w_inter = jnp.exp(inter - m_i)
        qkm = _dot_nt(q, k) * jnp.exp(dlog - m_i)
        cm = c_ref[0, 0, h]
        nm = n_ref[0, 0, h]
        num = _dot(qkm, v) + w_inter * _dot(q, cm)
        den = (jnp.sum(qkm, axis=1, keepdims=True)
               + w_inter * jnp.sum(q * nm, axis=1, keepdims=True))
        den = jnp.maximum(jnp.abs(den), jnp.exp(-m_i))
        hh = num / den
        m_new = m_i[c - 1:c, :]
        f_last = fcum_c[c - 1:c, :]
        w_j = jnp.exp(f_last - fcum_c + ig_c - m_new)
        decay = jnp.exp(f_last + m_prev - m_new)
        kw = w_j * k
        c_ref[0, 0, h] = decay * cm + _dot_tn(kw, v)
        n_ref[0, 0, h] = decay * nm + jnp.sum(kw, axis=0, keepdims=True)
        m_ref[0, 0, h] = jnp.broadcast_to(m_new, (1, LANES))
        hm = _sigmoid(og_ref[0, :, vs]) * hh
        o_ref[0, :, WIDTH + h * DV:WIDTH + (h + 1) * DV] = (
            _rms(hm) * mn_ref[:, vs] * _silu(z_ref[0, :, vs])).astype(BF16)


N_AB_IN = 16
N_AB_INIT = 5
N_AB_STATE_OUT = 4


def _ab_kernel(*refs, c, c_true, valid, has_init, has_prev):
    refs = list(refs)
    (q_ref, k_ref, v_ref, z_ref, sm_ref, wa_ref, ba_ref, gn_ref,
     qkm_ref, vm_ref, og_ref, zm_ref, gt_ref, cw_ref, cb_ref, mn_ref) = refs[:N_AB_IN]
    refs = refs[N_AB_IN:]
    s0_ref = init_refs = None
    if has_init:
        s0_ref = refs[0]
        init_refs = refs[1:N_AB_INIT]
        refs = refs[N_AB_INIT:]
    if has_prev:
        refs = refs[N_AB_STATE_OUT:]
    o_ref, s_ref, c_ref, n_ref, m_ref, buf = refs
    _gla_body(q_ref, k_ref, v_ref, z_ref, sm_ref, wa_ref, ba_ref, gn_ref, s0_ref, o_ref, s_ref,
              c=c, c_true=c_true, valid=valid)
    _mlstm_body(qkm_ref, vm_ref, og_ref, zm_ref, sm_ref, gt_ref, cw_ref, cb_ref, mn_ref, init_refs,
                o_ref, c_ref, n_ref, m_ref, buf, c=c, valid=valid)


def _ab_scan(main, small, gates_t, P, init, prev, *, layer, n_layers, c, c_true, valid):
    bsz, L, _ = main.shape
    nc = L // c
    has_init = init is not None
    has_prev = prev is not None
    kern = functools.partial(_ab_kernel, c=c, c_true=c_true, valid=valid,
                             has_init=has_init, has_prev=has_prev)
    base = (2 * QK + 2 * WIDTH) // WIDTH
    row = lambda blk: (lambda b, n: (b, n, blk))
    const = lambda b, n: (0, 0)
    state = lambda b, n: (layer, b, 0, 0, 0)
    in_specs = [
        pl.BlockSpec((1, c, QK), row(0)),
        pl.BlockSpec((1, c, QK), row(1)),
        pl.BlockSpec((1, c, WIDTH), row(2 * QK // WIDTH)),
        pl.BlockSpec((1, c, WIDTH), row(2 * QK // WIDTH + 1)),
        pl.BlockSpec((1, c, LANES), row(0)),
        pl.BlockSpec((3 * GLA_LOWRANK, QK), const),
        pl.BlockSpec((1, QK), const),
        pl.BlockSpec((1, WIDTH), const),
        pl.BlockSpec((1, c, 2 * QK), row(base)),
        pl.BlockSpec((1, c, WIDTH), row(base + 1)),
        pl.BlockSpec((1, c, WIDTH), row(base + 2)),
        pl.BlockSpec((1, c, WIDTH), row(base + 3)),
        pl.BlockSpec((1, 1, 2 * HEADS, c), lambda b, n: (b, n, 0, 0)),
        pl.BlockSpec((CONV, 2 * QK), const),
        pl.BlockSpec((1, 2 * QK), const),
        pl.BlockSpec((1, WIDTH), const),
    ]
    args = [main, main, main, main, small, P['wa3'], P['ba'], P['gnorm'],
            main, main, main, main, gates_t, P['cw'], P['cb'], P['mnorm']]
    assert len(args) == N_AB_IN
    if has_init:
        s0, c0, n0, m0, conv0 = init
        in_specs += [
            pl.BlockSpec((1, 1, HEADS, DK, DV), state),
            pl.BlockSpec((1, 1, HEADS, DK, DV), state),
            pl.BlockSpec((1, 1, HEADS, 1, DK), state),
            pl.BlockSpec((1, 1, HEADS, 1, 1), state),
            pl.BlockSpec((1, 1, HIST, 2 * QK), lambda b, n: (layer, b, 0, 0)),
        ]
        args += [s0, c0, n0, m0, conv0]
    aliases = {}
    if has_prev:
        for i, p in enumerate(prev):
            aliases[len(args)] = 1 + i
            in_specs.append(pl.BlockSpec(memory_space=pl.ANY))
            args.append(p)
    return pl.pallas_call(
        kern,
        out_shape=(jax.ShapeDtypeStruct((bsz, L, 2 * WIDTH), BF16),
                   jax.ShapeDtypeStruct((n_layers, bsz, HEADS, DK, DV), F32),
                   jax.ShapeDtypeStruct((n_layers, bsz, HEADS, DK, DV), F32),
                   jax.ShapeDtypeStruct((n_layers, bsz, HEADS, 1, DK), F32),
                   jax.ShapeDtypeStruct((n_layers, bsz, HEADS, 1, LANES), F32)),
        grid=(bsz, nc),
        in_specs=in_specs,
        out_specs=(pl.BlockSpec((1, c, 2 * WIDTH), lambda b, n: (b, n, 0)),
                   pl.BlockSpec((1, 1, HEADS, DK, DV), state),
                   pl.BlockSpec((1, 1, HEADS, DK, DV), state),
                   pl.BlockSpec((1, 1, HEADS, 1, DK), state),
                   pl.BlockSpec((1, 1, HEADS, 1, LANES), state)),
        scratch_shapes=[pltpu.VMEM((SUBLANES + c, 2 * QK), F32)],
        input_output_aliases=aliases,
        compiler_params=_cparams(("parallel", "arbitrary")),
        name="ab_scan",
    )(*args)


def _pad_rows(a, rows):
    if a.shape[0] == rows:
        return a
    return jnp.concatenate([a, jnp.zeros((rows - a.shape[0], a.shape[1]), a.dtype)], axis=0)


def _ssd_kernel(*refs, c, valid, has_init, has_prev, transposed, last):
    refs = list(refs)
    (z_ref, xp_ref, bcp_ref, sm_ref, cwx_ref, cwbc_ref, cbx_ref, cbbc_ref,
     alog_ref, d_ref, nw_ref) = refs[:11]
    refs = refs[11:]
    if has_init:
        s0_ref, hx_ref, hbc_ref = refs[:3]
        refs = refs[3:]
    if has_prev:
        refs = refs[1:]
    if transposed:
        o_ref, s_ref, xbuf, bcbuf, st_ref = refs
    else:
        o_ref, s_ref, xbuf, bcbuf = refs

    @pl.when(pl.program_id(1) == 0)
    def _():
        if has_init:
            _init_hist(xbuf, hx_ref[0, 0])
            _init_hist(bcbuf, hbc_ref[0, 0])
            if transposed:
                for g in range(SSD_G):
                    st_ref[g] = jnp.transpose(s0_ref[0, 0, g])
            else:
                s_ref[0, 0] = s0_ref[0, 0]
        else:
            _init_hist(xbuf, jnp.zeros((HIST, D_INNER), F32))
            _init_hist(bcbuf, jnp.zeros((HIST, SSD_BC), F32))
            if transposed:
                st_ref[...] = jnp.zeros((SSD_G, SSD_N, SSD_GW), F32)
            else:
                s_ref[0, 0] = jnp.zeros((SSD_G, SSD_GW, SSD_N), F32)

    xall = _silu(_causal_conv(xbuf, xp_ref[0], cwx_ref, cbx_ref, c))
    bcall = _silu(_causal_conv(bcbuf, bcp_ref[0], cwbc_ref, cbbc_ref, c))
    dt_all = _softplus(sm_ref[0][:, :SSD_HEADS])
    if valid < c:
        dt_all = jnp.where(_row_valid(c, valid), dt_all, 0.0)
    a_all = -jnp.exp(alog_ref[...])
    tri = _tri(c).astype(F32)
    acum_all = _dot_m01(tri, dt_all * a_all)

    e = (lax.shift_right_logical(lax.broadcasted_iota(jnp.int32, (SSD_HPG, SSD_GW), 1), SLOT_LOG2)
         == lax.broadcasted_iota(jnp.int32, (SSD_HPG, SSD_GW), 0)).astype(F32)
    rowi = lax.broadcasted_iota(jnp.int32, (c, SSD_GW), 0)
    jpos = jnp.bitwise_and(lax.broadcasted_iota(jnp.int32, (c, SSD_GW), 1), SLOT - 1)
    causal = jpos <= rowi
    diag = jpos == rowi
    sel = (lax.shift_right_logical(lax.broadcasted_iota(jnp.int32, (2 * SLOT, 2 * SSD_P), 0), SLOT_LOG2)
           == lax.shift_right_logical(lax.broadcasted_iota(jnp.int32, (2 * SLOT, 2 * SSD_P), 1), SLOT_LOG2))

    for g in range(SSD_G):
        gs = slice(g * SSD_GW, (g + 1) * SSD_GW)
        hs = slice(g * SSD_HPG, (g + 1) * SSD_HPG)
        x = xall[:, gs]
        bm = bcall[:, g * SSD_N:(g + 1) * SSD_N]
        cm = bcall[:, SSD_G * SSD_N + g * SSD_N:SSD_G * SSD_N + (g + 1) * SSD_N]
        ex = _dot_x01(jnp.concatenate([acum_all[:, hs], dt_all[:, hs]], axis=0), e)
        a_exp = ex[:c]
        dt_exp = ex[c:]
        a_row = jnp.sum(jnp.where(diag, a_exp, 0.0), axis=0, keepdims=True)
        dec = jnp.exp(jnp.where(causal, a_exp - a_row, NEG))
        bm_t = jnp.concatenate([_pad_rows(bm, SLOT)] * SSD_HPG, axis=0)
        m = (_dot_nt(cm, bm_t) * dec).astype(BF16)
        xdt = _pad_rows(x * dt_exp, SLOT).astype(BF16)
        ys = []
        for pr in range(SSD_HPG // 2):
            ps = slice(pr * 2 * SSD_P, (pr + 1) * 2 * SSD_P)
            xp = xdt[:, ps]
            xblk = jnp.where(sel, jnp.concatenate([xp, xp], axis=0), 0.0)
            ys.append(jnp.dot(m[:, ps], xblk, preferred_element_type=F32))
        y = jnp.concatenate(ys, axis=1)
        a_last = a_exp[c - 1:c, :]
        wx = x * (jnp.exp(a_last - a_exp) * dt_exp)
        if transposed:
            st = st_ref[g]
            y = y + _dot(cm, st) * jnp.exp(a_exp)
            st_ref[g] = st * jnp.exp(a_last) + _dot_tn(bm, wx)
        else:
            s = s_ref[0, 0, g]
            y = y + _dot_nt(cm, s) * jnp.exp(a_exp)
            upd = _dot_tn(wx, bm)
            sdec = jnp.exp(acum_all[c - 1:c, hs])
            for hh in range(SSD_HPG):
                rs = slice(hh * SSD_P, (hh + 1) * SSD_P)
                s_ref[0, 0, g, rs, :] = s[rs, :] * sdec[:, hh:hh + 1] + upd[rs, :]
        y = y + d_ref[:, gs] * x
        y = y * _silu(z_ref[0, :, gs])
        o_ref[0, :, gs] = (_rms(y) * nw_ref[:, gs]).astype(BF16)

    if transposed:
        @pl.when(pl.program_id(1) == last)
        def _():
            for g in range(SSD_G):
                s_ref[0, 0, g] = jnp.transpose(st_ref[g])


def _ssd(main, small, cw, cb, alog, d_exp, normw, init, prev, *, layer, n_layers, c, valid,
         transposed):
    bsz, L, _ = main.shape
    nc = L // c
    has_init = init is not None
    has_prev = prev is not None
    kern = functools.partial(_ssd_kernel, c=c, valid=valid, has_init=has_init, has_prev=has_prev,
                             transposed=transposed, last=nc - 1)
    bc_blk = 2 * D_INNER // SSD_BC
    in_specs = [
        pl.BlockSpec((1, c, D_INNER), lambda b, n: (b, n, 0)),
        pl.BlockSpec((1, c, D_INNER), lambda b, n: (b, n, 1)),
        pl.BlockSpec((1, c, SSD_BC), lambda b, n: (b, n, bc_blk)),
        pl.BlockSpec((1, c, LANES), lambda b, n: (b, n, 0)),
        pl.BlockSpec((CONV, D_INNER), lambda b, n: (0, 0)),
        pl.BlockSpec((CONV, SSD_BC), lambda b, n: (0, D_INNER // SSD_BC)),
        pl.BlockSpec((1, D_INNER), lambda b, n: (0, 0)),
        pl.BlockSpec((1, SSD_BC), lambda b, n: (0, D_INNER // SSD_BC)),
        pl.BlockSpec((1, SSD_HEADS), lambda b, n: (0, 0)),
        pl.BlockSpec((1, D_INNER), lambda b, n: (0, 0)),
        pl.BlockSpec((1, D_INNER), lambda b, n: (0, 0)),
    ]
    args = [main, main, main, small, cw, cw, cb, cb, alog, d_exp, normw]
    if has_init:
        s0, conv0 = init
        in_specs += [
            pl.BlockSpec((1, 1, SSD_G, SSD_GW, SSD_N), lambda b, n: (layer, b, 0, 0, 0)),
            pl.BlockSpec((1, 1, HIST, D_INNER), lambda b, n: (layer, b, 0, 0)),
            pl.BlockSpec((1, 1, HIST, SSD_BC), lambda b, n: (layer, b, 0, D_INNER // SSD_BC)),
        ]
        args += [s0, conv0, conv0]
    aliases = {}
    if has_prev:
        aliases = {len(args): 1}
        in_specs.append(pl.BlockSpec(memory_space=pl.ANY))
        args.append(prev)
    scratch = [pltpu.VMEM((SUBLANES + c, D_INNER), F32), pltpu.VMEM((SUBLANES + c, SSD_BC), F32)]
    if transposed:
        scratch.append(pltpu.VMEM((SSD_G, SSD_N, SSD_GW), F32))
    return pl.pallas_call(
        kern,
        out_shape=(jax.ShapeDtypeStruct((bsz, L, D_INNER), BF16),
                   jax.ShapeDtypeStruct((n_layers, bsz, SSD_G, SSD_GW, SSD_N), F32)),
        grid=(bsz, nc),
        in_specs=in_specs,
        out_specs=(pl.BlockSpec((1, c, D_INNER), lambda b, n: (b, n, 0)),
                   pl.BlockSpec((1, 1, SSD_G, SSD_GW, SSD_N), lambda b, n: (layer, b, 0, 0, 0))),
        scratch_shapes=scratch,
        input_output_aliases=aliases,
        compiler_params=_cparams(("parallel", "arbitrary")),
        name="ssd_scan",
    )(*args)


def _prep_params(p):
    n_ab = p['w_in_ab'].shape[0]
    n_c = p['w_in_c'].shape[0]
    o_ag = 2 * QK + 2 * WIDTH
    o_m = o_ag + GLA_LOWRANK
    o_if = o_m + 2 * QK + 3 * WIDTH
    ab, cc = [], []
    w_main_ab = _cast_weights(p['w_in_ab'], ((0, o_ag, 0), (o_m, o_if - o_m, o_ag)), tr=128)
    w_main_c = _cast_weights(p['w_in_c'], ((0, D_INNER + SSD_CONV_DIM, 0),), tr=128)
    w_out_ab = _cast_weights(p['w_out_ab'], ((0, D_MODEL, 0),), tr=512)
    w_out_c = _cast_weights(p['w_out_c'], ((0, D_MODEL, 0),), tr=512)
    for i in range(n_ab):
        w = p['w_in_ab'][i]
        w_small = jnp.zeros((D_MODEL, LANES), F32)
        w_small = w_small.at[:, :GLA_LOWRANK].set(w[:, o_ag:o_m])
        w_small = w_small.at[:, SM_I:SM_I + 2 * HEADS].set(w[:, o_if:])
        b_small = jnp.zeros((1, LANES), F32)
        b_small = b_small.at[0, SM_I:SM_I + HEADS].set(p['mlstm_b_i'][i])
        b_small = b_small.at[0, SM_F:SM_F + HEADS].set(p['mlstm_b_f'][i])
        wa = p['gla_w_a2'][i]
        wa_hi = wa.astype(BF16)
        wa_lo = (wa - wa_hi.astype(F32)).astype(BF16)
        ab.append(dict(
            norm=p['norm_ab'][i][None], w_main=w_main_ab, w_small=w_small, b_small=b_small,
            wa3=jnp.concatenate([wa_hi, wa_lo, wa_hi], axis=0),
            ba=p['gla_b_a'][i][None], gnorm=p['gla_norm'][i][None],
            cw=p['mlstm_conv_w'][i], cb=p['mlstm_conv_b'][i][None],
            mnorm=p['mlstm_norm'][i][None], w_out=w_out_ab))
    o_dt = D_INNER + SSD_CONV_DIM
    for i in range(n_c):
        w = p['w_in_c'][i]
        w_small = jnp.zeros((D_MODEL, LANES), F32).at[:, :SSD_HEADS].set(w[:, o_dt:])
        b_small = jnp.zeros((1, LANES), F32).at[0, :SSD_HEADS].set(p['ssd_dt_bias'][i])
        cc.append(dict(
            norm=p['norm_c'][i][None], w_main=w_main_c, w_small=w_small,
            b_small=b_small, cw=p['ssd_conv_w'][i], cb=p['ssd_conv_b'][i][None],
            alog=p['ssd_a_log'][i][None],
            d_exp=jnp.repeat(p['ssd_d'][i], SSD_P)[None],
            normw=p['ssd_norm'][i][None], w_out=w_out_c))
    return ab, cc


def _trunk(x, states, ab, cc, final_w, *, c, c_true, valid, tm, transposed):
    bsz, L, _ = x.shape
    t = bsz * L
    nc = L // c
    n_ab, n_c = len(ab), len(cc)
    xt = x.reshape(t, D_MODEL)
    gla_o = mc_o = mn_o = mm_o = ssm_o = None
    mconv, sconv = [], []
    row0 = (nc - 1) * c + valid - HIST
    for layer in range(n_ab + n_c):
        i = layer // 2
        if layer % 2 == 0:
            P = ab[i]
            main, small = _inproj(xt, P['norm'], P['w_main'], i, P['w_small'], P['b_small'],
                                  tm=tm, tn=1024)
            main = main.reshape(bsz, L, -1)
            small = small.reshape(bsz, L, LANES)
            gates_t = jnp.swapaxes(
                small[:, :, SM_I:SM_I + 2 * HEADS].reshape(bsz, nc, c, 2 * HEADS), 2, 3)
            init = None if states is None else (states['gla'], states['mc'], states['mn'],
                                                states['mm'], states['mconv'])
            prev = None if gla_o is None else (gla_o, mc_o, mn_o, mm_o)
            y, gla_o, mc_o, mn_o, mm_o = _ab_scan(main, small, gates_t, P, init, prev, layer=i,
                                                  n_layers=n_ab, c=c, c_true=c_true, valid=valid)
            mconv.append(main[:, row0:row0 + HIST, 2 * QK + 2 * WIDTH:2 * QK + 2 * WIDTH + 2 * QK])
            y2 = y.reshape(t, D_INNER)
            xt = _outproj(y2, y2, 0, 1, P['w_out'], i, xt, tm=min(tm, 512), tn=512)
        else:
            P = cc[i]
            main, small = _inproj(xt, P['norm'], P['w_main'], i, P['w_small'], P['b_small'],
                                  tm=tm, tn=1024)
            main = main.reshape(bsz, L, -1)
            small = small.reshape(bsz, L, LANES)
            init = None if states is None else (states['ssm'], states['sconv'])
            y, ssm_o = _ssd(main, small, P['cw'], P['cb'], P['alog'], P['d_exp'], P['normw'],
                            init, ssm_o, layer=i, n_layers=n_c, c=c, valid=valid,
                            transposed=transposed)
            sconv.append(main[:, row0:row0 + HIST, D_INNER:D_INNER + SSD_CONV_DIM])
            y2 = y.reshape(t, D_INNER)
            xt = _outproj(y2, y2, 0, 1, P['w_out'], i, xt, tm=min(tm, 512), tn=512)
    y = _final_norm(xt, final_w, tm=min(tm, 512)).reshape(bsz, L, D_MODEL)
    return (y, gla_o, mc_o, mn_o[:, :, :, 0, :], mm_o[:, :, :, 0, 0],
            jnp.stack(mconv), ssm_o.reshape(n_c, bsz, SSD_HEADS, SSD_P, SSD_N), jnp.stack(sconv))


def kernel(x_prompt, x_sample, state_gla, state_mlstm_c, state_mlstm_n, state_mlstm_m, state_mlstm_conv, state_ssm, state_ssm_conv, norm_ab, w_in_ab, gla_w_a2, gla_b_a, gla_norm, mlstm_conv_w, mlstm_conv_b, mlstm_b_i, mlstm_b_f, mlstm_norm, w_out_ab, norm_c, w_in_c, ssd_conv_w, ssd_conv_b, ssd_dt_bias, ssd_a_log, ssd_d, ssd_norm, w_out_c, final_norm):
    p = dict(norm_ab=norm_ab, w_in_ab=w_in_ab, gla_w_a2=gla_w_a2, gla_b_a=gla_b_a, gla_norm=gla_norm,
             mlstm_conv_w=mlstm_conv_w, mlstm_conv_b=mlstm_conv_b, mlstm_b_i=mlstm_b_i,
             mlstm_b_f=mlstm_b_f, mlstm_norm=mlstm_norm, w_out_ab=w_out_ab, norm_c=norm_c,
             w_in_c=w_in_c, ssd_conv_w=ssd_conv_w, ssd_conv_b=ssd_conv_b, ssd_dt_bias=ssd_dt_bias,
             ssd_a_log=ssd_a_log, ssd_d=ssd_d, ssd_norm=ssd_norm, w_out_c=w_out_c)
    ab, cc = _prep_params(p)
    fw = final_norm[None]
    n_ab, dec_b = state_gla.shape[0], state_gla.shape[1]
    n_c = state_ssm.shape[0]
    dec_l = x_sample.shape[1]

    yp, gla_p, mc_p, mn_p, mm_p, mcv_p, ssm_p, scv_p = _trunk(
        x_prompt, None, ab, cc, fw, c=CHUNK, c_true=CHUNK, valid=CHUNK, tm=1024, transposed=True)

    xs = jnp.pad(x_sample, ((0, 0), (0, SUBLANES - dec_l), (0, 0)))
    states = dict(
        gla=state_gla, mc=state_mlstm_c,
        mn=state_mlstm_n.reshape(n_ab, dec_b, HEADS, 1, DK),
        mm=state_mlstm_m.reshape(n_ab, dec_b, HEADS, 1, 1),
        mconv=state_mlstm_conv,
        ssm=state_ssm.reshape(n_c, dec_b, SSD_G, SSD_GW, SSD_N),
        sconv=state_ssm_conv)
    ys, gla_s, mc_s, mn_s, mm_s, mcv_s, ssm_s, scv_s = _trunk(
        xs, states, ab, cc, fw, c=SUBLANES, c_true=dec_l, valid=dec_l, tm=1024, transposed=False)
    ys = ys[:, :dec_l]
    return (yp, ys, gla_p, gla_s, mc_p, mc_s, mn_p, mn_s, mm_p, mm_s,
            mcv_p, mcv_s, ssm_p, ssm_s, scv_p, scv_s)
```

```python
import functools

import jax
import jax.numpy as jnp
from jax import lax
from jax.experimental import pallas as pl
from jax.experimental.pallas import tpu as pltpu

F32 = jnp.float32
BF16 = jnp.bfloat16

D_MODEL = 2048
D_INNER = 4096
CHUNK = 64
EPS = 1e-6
NEG = -1e30
HEADS = 4
DK = 256
DV = 512
QK = HEADS * DK
WIDTH = HEADS * DV
GLA_LOWRANK = 16
GLA_GATE_TEMP = 16.0
CONV = 4
HIST = CONV - 1
SM_I = 16
SM_F = 20
SSD_HEADS = 64
SSD_P = 64
SSD_G = 8
SSD_HPG = 8
SSD_N = 128
SSD_GW = SSD_HPG * SSD_P
SSD_BC = 2 * SSD_G * SSD_N
SSD_CONV_DIM = D_INNER + SSD_BC
LANES = 128
SUBLANES = 8
SLOT = 64
SLOT_LOG2 = 6

VMEM_LIMIT = 56 * 1024 * 1024

def _cparams(sem):
    return pltpu.CompilerParams(dimension_semantics=sem, vmem_limit_bytes=VMEM_LIMIT)


def _dot(a, b):
    return jnp.dot(a.astype(BF16), b.astype(BF16), preferred_element_type=F32)


def _dot_nt(a, b):
    return lax.dot_general(a.astype(BF16), b.astype(BF16), (((1,), (1,)), ((), ())),
                           preferred_element_type=F32)


def _dot_tn(a, b):
    return lax.dot_general(a.astype(BF16), b.astype(BF16), (((0,), (0,)), ((), ())),
                           preferred_element_type=F32)


def _split3(x):
    x1 = x.astype(BF16).astype(F32)
    r = x - x1
    x2 = r.astype(BF16).astype(F32)
    x3 = (r - x2).astype(BF16).astype(F32)
    return x1, x2, x3


def _dot_m01(m01, x):
    return _dot(jnp.concatenate([m01] * 3, axis=1), jnp.concatenate(_split3(x), axis=0))


def _dot_x01(x, m01):
    return _dot(jnp.concatenate(_split3(x), axis=1), jnp.concatenate([m01] * 3, axis=0))


def _sigmoid(x):
    return 1.0 / (1.0 + jnp.exp(-x))


def _silu(x):
    return x * _sigmoid(x)


def _softplus(x):
    return jnp.maximum(x, 0.0) + jnp.log(1.0 + jnp.exp(-jnp.abs(x)))


def _log_sigmoid(x):
    return -_softplus(-x)


def _rms(x):
    return x * lax.rsqrt(jnp.mean(x * x, axis=-1, keepdims=True) + EPS)


def _tri(c):
    row = lax.broadcasted_iota(jnp.int32, (c, c), 0)
    col = lax.broadcasted_iota(jnp.int32, (c, c), 1)
    return col <= row


def _row_valid(c, valid):
    return lax.broadcasted_iota(jnp.int32, (c, 1), 0) < valid


def _col_valid(c, valid):
    return lax.broadcasted_iota(jnp.int32, (1, c), 1) < valid


NORM_ROWS = 256


def _inproj_kernel(x_ref, nw_ref, w_ref, ws_ref, bs_ref, o_ref, os_ref, xn_ref):
    @pl.when(pl.program_id(1) == 0)
    def _():
        ws = ws_ref[...]
        wh = ws.astype(BF16)
        wl = (ws - wh.astype(F32)).astype(BF16)
        wcat = jnp.concatenate([wh, wl], axis=1)
        nr = min(NORM_ROWS, x_ref.shape[0])
        for r in range(x_ref.shape[0] // nr):
            rows = pl.ds(r * nr, nr)
            xn = _rms(x_ref[rows, :]) * nw_ref[...]
            xh = xn.astype(BF16)
            xl = (xn - xh.astype(F32)).astype(BF16)
            xn_ref[rows, :] = xh
            p = jnp.dot(xh, wcat, preferred_element_type=F32)
            os_ref[rows, :] = (p[:, :LANES] + p[:, LANES:]
                               + jnp.dot(xl, wh, preferred_element_type=F32) + bs_ref[...])

    o_ref[...] = jnp.dot(xn_ref[...], w_ref[...], preferred_element_type=F32)


def _inproj(x, norm_w, w_main, layer, w_small, b_small, *, tm, tn):
    t, d = x.shape
    n = w_main.shape[2]
    return pl.pallas_call(
        _inproj_kernel,
        out_shape=(jax.ShapeDtypeStruct((t, n), F32), jax.ShapeDtypeStruct((t, LANES), F32)),
        grid=(t // tm, n // tn),
        in_specs=[
            pl.BlockSpec((tm, d), lambda i, j: (i, 0)),
            pl.BlockSpec((1, d), lambda i, j: (0, 0)),
            pl.BlockSpec((None, d, tn), lambda i, j: (layer, 0, j)),
            pl.BlockSpec((d, LANES), lambda i, j: (0, 0)),
            pl.BlockSpec((1, LANES), lambda i, j: (0, 0)),
        ],
        out_specs=(pl.BlockSpec((tm, tn), lambda i, j: (i, j)),
                   pl.BlockSpec((tm, LANES), lambda i, j: (i, 0))),
        scratch_shapes=[pltpu.VMEM((tm, d), BF16)],
        compiler_params=_cparams(("parallel", "arbitrary")),
        name="inproj",
    )(x, norm_w, w_main, w_small, b_small)


CAST_COLS = 1024


def _cast_kernel(w_ref, o_ref, *, segs):
    for src, width, dst in segs:
        for off in range(0, width, CAST_COLS):
            n = min(CAST_COLS, width - off)
            o_ref[0, :, dst + off:dst + off + n] = w_ref[0, :, src + off:src + off + n].astype(BF16)


def _cast_weights(w, segs, *, tr):
    nl, rows, cols = w.shape
    n_out = sum(width for _, width, _ in segs)
    return pl.pallas_call(
        functools.partial(_cast_kernel, segs=segs),
        out_shape=jax.ShapeDtypeStruct((nl, rows, n_out), BF16),
        grid=(nl, rows // tr),
        in_specs=[pl.BlockSpec((1, tr, cols), lambda l, i: (l, i, 0))],
        out_specs=pl.BlockSpec((1, tr, n_out), lambda l, i: (l, i, 0)),
        compiler_params=_cparams(("parallel", "parallel")),
        name="cast_weights",
    )(w)


CAST_T_COLS = 1024


def _cast_t_kernel(a_ref, b_ref, o_ref, *, shift_from, shift):
    def emit(s):
        for k in range(CAST_T_COLS // LANES):
            lo = s + k * LANES
            if lo + LANES <= CAST_T_COLS:
                blk = a_ref[0, lo:lo + LANES, :]
            else:
                blk = jnp.concatenate([a_ref[0, lo:CAST_T_COLS, :],
                                       b_ref[0, 0:lo + LANES - CAST_T_COLS, :]], axis=0)
            o_ref[0, :, k * LANES:(k + 1) * LANES] = jnp.transpose(blk).astype(BF16)

    if shift == 0:
        emit(0)
    else:
        @pl.when(pl.program_id(1) < shift_from)
        def _():
            emit(0)

        @pl.when(pl.program_id(1) >= shift_from)
        def _():
            emit(shift)


def _cast_weights_t(wt, n_out, *, shift_from=0, shift=0):
    nl, _, d = wt.shape
    per = CAST_T_COLS // LANES
    return pl.pallas_call(
        functools.partial(_cast_t_kernel, shift_from=shift_from, shift=shift),
        out_shape=jax.ShapeDtypeStruct((nl, d, n_out), BF16),
        grid=(nl, n_out // CAST_T_COLS),
        in_specs=[pl.BlockSpec((1, CAST_T_COLS, d), lambda l, j: (l, j, 0)),
                  pl.BlockSpec((1, LANES, d), lambda l, j: (l, (j + 1) * per, 0))],
        out_specs=pl.BlockSpec((1, d, CAST_T_COLS), lambda l, j: (l, 0, j)),
        compiler_params=_cparams(("parallel", "parallel")),
        name="cast_weights_t",
    )(wt, wt)


def _outproj_kernel(a_ref, b_ref, wa_ref, wb_ref, x_ref, o_ref):
    o_ref[...] = (x_ref[...]
                  + jnp.dot(a_ref[...], wa_ref[...], preferred_element_type=F32)
                  + jnp.dot(b_ref[...], wb_ref[...], preferred_element_type=F32))


def _outproj(a, b, a_blk, b_blk, w, layer, x, *, tm, tn):
    t, d = x.shape
    half = D_INNER // 2
    return pl.pallas_call(
        _outproj_kernel,
        out_shape=jax.ShapeDtypeStruct((t, d), F32),
        grid=(t // tm, d // tn),
        in_specs=[
            pl.BlockSpec((tm, half), lambda i, j: (i, a_blk)),
            pl.BlockSpec((tm, half), lambda i, j: (i, b_blk)),
            pl.BlockSpec((None, half, tn), lambda i, j: (layer, 0, j)),
            pl.BlockSpec((None, half, tn), lambda i, j: (layer, 1, j)),
            pl.BlockSpec((tm, tn), lambda i, j: (i, j)),
        ],
        out_specs=pl.BlockSpec((tm, tn), lambda i, j: (i, j)),
        compiler_params=_cparams(("parallel", "arbitrary")),
        name="outproj",
    )(a, b, w, w, x)


def _final_norm_kernel(x_ref, w_ref, o_ref):
    o_ref[...] = _rms(x_ref[...]) * w_ref[...]


def _final_norm(x, w, *, tm):
    t, d = x.shape
    return pl.pallas_call(
        _final_norm_kernel,
        out_shape=jax.ShapeDtypeStruct((t, d), F32),
        grid=(t // tm,),
        in_specs=[pl.BlockSpec((tm, d), lambda i: (i, 0)),
                  pl.BlockSpec((1, d), lambda i: (0, 0))],
        out_specs=pl.BlockSpec((tm, d), lambda i: (i, 0)),
        compiler_params=_cparams(("parallel",)),
        name="final_norm",
    )(x, w)


def _causal_conv(buf_ref, x, w_ref, b_ref, c):
    buf_ref[pl.ds(SUBLANES, c), :] = x
    xp = buf_ref[...]
    out = b_ref[...] + x * w_ref[CONV - 1:CONV, :]
    for s in range(1, CONV):
        out = out + pltpu.roll(xp, s, 0)[SUBLANES:, :] * w_ref[CONV - 1 - s:CONV - s, :]
    buf_ref[pl.ds(SUBLANES - HIST, HIST), :] = buf_ref[pl.ds(SUBLANES + c - HIST, HIST), :]
    return out


def _init_hist(buf_ref, hist):
    buf_ref[pl.ds(SUBLANES - HIST, HIST), :] = hist


def _gla_body(q_ref, k_ref, v_ref, z_ref, sm_ref, wa_ref, ba_ref, gn_ref, s0_ref, o_ref, s_ref,
              *, c, c_true, valid):
    @pl.when(pl.program_id(1) == 0)
    def _():
        if s0_ref is not None:
            s_ref[0, 0] = s0_ref[0, 0]
        else:
            s_ref[0, 0] = jnp.zeros((HEADS, DK, DV), F32)

    ag = sm_ref[0][:, :GLA_LOWRANK]
    a1 = ag.astype(BF16).astype(F32)
    a2 = (ag - a1).astype(BF16).astype(F32)
    gpre = _dot(jnp.concatenate([a1, a1, a2], axis=1), wa_ref[...]) + ba_ref[...]
    g = _log_sigmoid(gpre) * (1.0 / GLA_GATE_TEMP)
    rv = _row_valid(c, valid) if valid < c else None
    if rv is not None:
        g = jnp.where(rv, g, 0.0)
    tri = _tri(c)
    g3 = jnp.concatenate(_split3(g), axis=0).astype(BF16)
    tri3 = jnp.concatenate([tri.astype(F32)] * 3, axis=1).astype(BF16)
    b = jnp.dot(tri3, g3, preferred_element_type=F32)
    dcol = jnp.exp(lax.dot_general(g3, jnp.ones((3 * c, LANES), BF16), (((0,), (0,)), ((), ())),
                                   preferred_element_type=F32))

    for h in range(HEADS):
        ks = slice(h * DK, (h + 1) * DK)
        vs = slice(h * DV, (h + 1) * DV)
        q = q_ref[0, :, ks] * (DK ** -0.5)
        k = k_ref[0, :, ks]
        if rv is not None:
            k = jnp.where(rv, k, 0.0)
        v = v_ref[0, :, vs]
        bh = b[:, ks]
        mid = bh[c_true // 2:c_true // 2 + 1, :]
        b_last = bh[c - 1:c, :]
        s = s_ref[0, 0, h]
        o = _dot(q * jnp.exp(bh), s)
        att = _dot_nt(q * jnp.exp(bh - mid), k * jnp.exp(mid - bh))
        o = o + _dot(jnp.where(tri, att, 0.0), v)
        dc = dcol[ks, :]
        s_ref[0, 0, h] = (s * jnp.concatenate([dc] * (DV // LANES), axis=1)
                          + _dot_tn(k * jnp.exp(b_last - bh), v))
        o_ref[0, :, vs] = (_rms(o) * gn_ref[:, vs] * _silu(z_ref[0, :, vs])).astype(BF16)


def _mlstm_body(qk_ref, v_ref, og_ref, z_ref, sm_ref, gt_ref, cw_ref, cb_ref, mn_ref, init_refs,
                o_ref, c_ref, n_ref, m_ref, buf, *, c, valid):
    has_init = init_refs is not None
    if has_init:
        c0_ref, n0_ref, m0_ref, h0_ref = init_refs

    @pl.when(pl.program_id(1) == 0)
    def _():
        if has_init:
            c_ref[0, 0] = c0_ref[0, 0]
            n_ref[0, 0] = n0_ref[0, 0]
            m_ref[0, 0] = jnp.broadcast_to(m0_ref[0, 0], (HEADS, 1, LANES))
            _init_hist(buf, h0_ref[0, 0])
        else:
            c_ref[0, 0] = jnp.zeros((HEADS, DK, DV), F32)
            n_ref[0, 0] = jnp.zeros((HEADS, 1, DK), F32)
            m_ref[0, 0] = jnp.zeros((HEADS, 1, LANES), F32)
            _init_hist(buf, jnp.zeros((HIST, 2 * QK), F32))

    qk = _silu(_causal_conv(buf, qk_ref[0], cw_ref, cb_ref, c))
    sm = sm_ref[0]
    gt = gt_ref[0, 0]
    rv = _row_valid(c, valid) if valid < c else None
    cv = _col_valid(c, valid) if valid < c else None
    tri = _tri(c)
    row = lax.broadcasted_iota(jnp.int32, (c, c), 0)
    col = lax.broadcasted_iota(jnp.int32, (c, c), 1)
    triu = row <= col

    for h in range(HEADS):
        vs = slice(h * DV, (h + 1) * DV)
        q = qk[:, h * DK:(h + 1) * DK]
        k = qk[:, QK + h * DK:QK + (h + 1) * DK] * (DK ** -0.5)
        v = v_ref[0, :, vs]
        ig_c = sm[:, SM_I + h:SM_I + h + 1]
        lf_c = _log_sigmoid(sm[:, SM_F + h:SM_F + h + 1])
        ig_r = gt[h:h + 1, :]
        lf_r = _log_sigmoid(gt[HEADS + h:HEADS + h + 1, :])
        if rv is not None:
            lf_c = jnp.where(rv, lf_c, 0.0)
            lf_r = jnp.where(cv, lf_r, 0.0)
            ig_c = jnp.where(rv, ig_c, NEG)
            ig_r = jnp.where(cv, ig_r, NEG)
        fcum_c = jnp.sum(jnp.where(tri, lf_r, 0.0), axis=1, keepdims=True)
        fcum_r = jnp.sum(jnp.where(triu, lf_c, 0.0), axis=0, keepdims=True)
        m_prev = m_ref[0, 0, h][:, 0:1]
        dlog = jnp.where(tri, fcum_c - fcum_r + ig_r, NEG)
        inter = fcum_c + m_prev
        m_i = jnp.maximum(inter, jnp.max(dlog, axis=1, keepdims=True))
        w_inter = jnp.exp(inter - m_i)
        qkm = _dot_nt(q, k) * jnp.exp(dlog - m_i)
        cm = c_ref[0, 0, h]
        nm = n_ref[0, 0, h]
        num = _dot(qkm, v) + w_inter * _dot(q, cm)
        den = (jnp.sum(qkm, axis=1, keepdims=True)
               + w_inter * jnp.sum(q * nm, axis=1, keepdims=True))
        den = jnp.maximum(jnp.abs(den), jnp.exp(-m_i))
        hh = num / den
        m_new = m_i[c - 1:c, :]
        f_last = fcum_c[c - 1:c, :]
        w_j = jnp.exp(f_last - fcum_c + ig_c - m_new)
        decay = jnp.exp(f_last + m_prev - m_new)
        kw = w_j * k
        c_ref[0, 0, h] = decay * cm + _dot_tn(kw, v)
        n_ref[0, 0, h] = decay * nm + jnp.sum(kw, axis=0, keepdims=True)
        m_ref[0, 0, h] = jnp.broadcast_to(m_new, (1, LANES))
        hm = _sigmoid(og_ref[0, :, vs]) * hh
        o_ref[0, :, WIDTH + h * DV:WIDTH + (h + 1) * DV] = (
            _rms(hm) * mn_ref[:, vs] * _silu(z_ref[0, :, vs])).astype(BF16)


N_AB_IN = 16
N_AB_INIT = 5
N_AB_STATE_OUT = 4


def _ab_kernel(*refs, c, c_true, valid, has_init, has_prev):
    refs = list(refs)
    (q_ref, k_ref, v_ref, z_ref, sm_ref, wa_ref, ba_ref, gn_ref,
     qkm_ref, vm_ref, og_ref, zm_ref, gt_ref, cw_ref, cb_ref, mn_ref) = refs[:N_AB_IN]
    refs = refs[N_AB_IN:]
    s0_ref = init_refs = None
    if has_init:
        s0_ref = refs[0]
        init_refs = refs[1:N_AB_INIT]
        refs = refs[N_AB_INIT:]
    if has_prev:
        refs = refs[N_AB_STATE_OUT:]
    o_ref, s_ref, c_ref, n_ref, m_ref, buf = refs
    _gla_body(q_ref, k_ref, v_ref, z_ref, sm_ref, wa_ref, ba_ref, gn_ref, s0_ref, o_ref, s_ref,
              c=c, c_true=c_true, valid=valid)
    _mlstm_body(qkm_ref, vm_ref, og_ref, zm_ref, sm_ref, gt_ref, cw_ref, cb_ref, mn_ref, init_refs,
                o_ref, c_ref, n_ref, m_ref, buf, c=c, valid=valid)


def _ab_scan(main, small, gates_t, P, init, prev, *, layer, n_layers, c, c_true, valid):
    bsz, L, _ = main.shape
    nc = L // c
    has_init = init is not None
    has_prev = prev is not None
    kern = functools.partial(_ab_kernel, c=c, c_true=c_true, valid=valid,
                             has_init=has_init, has_prev=has_prev)
    base = (2 * QK + 2 * WIDTH) // WIDTH
    row = lambda blk: (lambda b, n: (b, n, blk))
    const = lambda b, n: (0, 0)
    state = lambda b, n: (layer, b, 0, 0, 0)
    in_specs = [
        pl.BlockSpec((1, c, QK), row(0)),
        pl.BlockSpec((1, c, QK), row(1)),
        pl.BlockSpec((1, c, WIDTH), row(2 * QK // WIDTH)),
        pl.BlockSpec((1, c, WIDTH), row(2 * QK // WIDTH + 1)),
        pl.BlockSpec((1, c, LANES), row(0)),
        pl.BlockSpec((3 * GLA_LOWRANK, QK), const),
        pl.BlockSpec((1, QK), const),
        pl.BlockSpec((1, WIDTH), const),
        pl.BlockSpec((1, c, 2 * QK), row(base)),
        pl.BlockSpec((1, c, WIDTH), row(base + 1)),
        pl.BlockSpec((1, c, WIDTH), row(base + 2)),
        pl.BlockSpec((1, c, WIDTH), row(base + 3)),
        pl.BlockSpec((1, 1, 2 * HEADS, c), lambda b, n: (b, n, 0, 0)),
        pl.BlockSpec((CONV, 2 * QK), const),
        pl.BlockSpec((1, 2 * QK), const),
        pl.BlockSpec((1, WIDTH), const),
    ]
    args = [main, main, main, main, small, P['wa3'], P['ba'], P['gnorm'],
            main, main, main, main, gates_t, P['cw'], P['cb'], P['mnorm']]
    assert len(args) == N_AB_IN
    if has_init:
        s0, c0, n0, m0, conv0 = init
        in_specs += [
            pl.BlockSpec((1, 1, HEADS, DK, DV), state),
            pl.BlockSpec((1, 1, HEADS, DK, DV), state),
            pl.BlockSpec((1, 1, HEADS, 1, DK), state),
            pl.BlockSpec((1, 1, HEADS, 1, 1), state),
            pl.BlockSpec((1, 1, HIST, 2 * QK), lambda b, n: (layer, b, 0, 0)),
        ]
        args += [s0, c0, n0, m0, conv0]
    aliases = {}
    if has_prev:
        for i, p in enumerate(prev):
            aliases[len(args)] = 1 + i
            in_specs.append(pl.BlockSpec(memory_space=pl.ANY))
            args.append(p)
    return pl.pallas_call(
        kern,
        out_shape=(jax.ShapeDtypeStruct((bsz, L, 2 * WIDTH), BF16),
                   jax.ShapeDtypeStruct((n_layers, bsz, HEADS, DK, DV), F32),
                   jax.ShapeDtypeStruct((n_layers, bsz, HEADS, DK, DV), F32),
                   jax.ShapeDtypeStruct((n_layers, bsz, HEADS, 1, DK), F32),
                   jax.ShapeDtypeStruct((n_layers, bsz, HEADS, 1, LANES), F32)),
        grid=(bsz, nc),
        in_specs=in_specs,
        out_specs=(pl.BlockSpec((1, c, 2 * WIDTH), lambda b, n: (b, n, 0)),
                   pl.BlockSpec((1, 1, HEADS, DK, DV), state),
                   pl.BlockSpec((1, 1, HEADS, DK, DV), state),
                   pl.BlockSpec((1, 1, HEADS, 1, DK), state),
                   pl.BlockSpec((1, 1, HEADS, 1, LANES), state)),
        scratch_shapes=[pltpu.VMEM((SUBLANES + c, 2 * QK), F32)],
        input_output_aliases=aliases,
        compiler_params=_cparams(("parallel", "arbitrary")),
        name="ab_scan",
    )(*args)


def _pad_rows(a, rows):
    if a.shape[0] == rows:
        return a
    return jnp.concatenate([a, jnp.zeros((rows - a.shape[0], a.shape[1]), a.dtype)], axis=0)


def _ssd_kernel(*refs, c, valid, has_init, has_prev, transposed, last):
    refs = list(refs)
    (z_ref, xp_ref, bcp_ref, sm_ref, cwx_ref, cwbc_ref, cbx_ref, cbbc_ref,
     alog_ref, d_ref, nw_ref) = refs[:11]
    refs = refs[11:]
    if has_init:
        s0_ref, hx_ref, hbc_ref = refs[:3]
        refs = refs[3:]
    if has_prev:
        refs = refs[1:]
    if transposed:
        o_ref, s_ref, xbuf, bcbuf, st_ref = refs
    else:
        o_ref, s_ref, xbuf, bcbuf = refs

    @pl.when(pl.program_id(1) == 0)
    def _():
        if has_init:
            _init_hist(xbuf, hx_ref[0, 0])
            _init_hist(bcbuf, hbc_ref[0, 0])
            if transposed:
                for g in range(SSD_G):
                    st_ref[g] = jnp.transpose(s0_ref[0, 0, g])
            else:
                s_ref[0, 0] = s0_ref[0, 0]
        else:
            _init_hist(xbuf, jnp.zeros((HIST, D_INNER), F32))
            _init_hist(bcbuf, jnp.zeros((HIST, SSD_BC), F32))
            if transposed:
                st_ref[...] = jnp.zeros((SSD_G, SSD_N, SSD_GW), F32)
            else:
                s_ref[0, 0] = jnp.zeros((SSD_G, SSD_GW, SSD_N), F32)

    xall = _silu(_causal_conv(xbuf, xp_ref[0], cwx_ref, cbx_ref, c))
    bcall = _silu(_causal_conv(bcbuf, bcp_ref[0], cwbc_ref, cbbc_ref, c))
    dt_all = _softplus(sm_ref[0][:, :SSD_HEADS])
    if valid < c:
        dt_all = jnp.where(_row_valid(c, valid), dt_all, 0.0)
    a_all = -jnp.exp(alog_ref[...])
    tri = _tri(c).astype(F32)
    acum_all = _dot_m01(tri, dt_all * a_all)

    e = (lax.shift_right_logical(lax.broadcasted_iota(jnp.int32, (SSD_HPG, SSD_GW), 1), SLOT_LOG2)
         == lax.broadcasted_iota(jnp.int32, (SSD_HPG, SSD_GW), 0)).astype(F32)
    rowi = lax.broadcasted_iota(jnp.int32, (c, SSD_GW), 0)
    jpos = jnp.bitwise_and(lax.broadcasted_iota(jnp.int32, (c, SSD_GW), 1), SLOT - 1)
    causal = jpos <= rowi
    diag = jpos == rowi
    sel = (lax.shift_right_logical(lax.broadcasted_iota(jnp.int32, (2 * SLOT, 2 * SSD_P), 0), SLOT_LOG2)
           == lax.shift_right_logical(lax.broadcasted_iota(jnp.int32, (2 * SLOT, 2 * SSD_P), 1), SLOT_LOG2))

    for g in range(SSD_G):
        gs = slice(g * SSD_GW, (g + 1) * SSD_GW)
        hs = slice(g * SSD_HPG, (g + 1) * SSD_HPG)
        x = xall[:, gs]
        bm = bcall[:, g * SSD_N:(g + 1) * SSD_N]
        cm = bcall[:, SSD_G * SSD_N + g * SSD_N:SSD_G * SSD_N + (g + 1) * SSD_N]
        ex = _dot_x01(jnp.concatenate([acum_all[:, hs], dt_all[:, hs]], axis=0), e)
        a_exp = ex[:c]
        dt_exp = ex[c:]
        a_row = jnp.sum(jnp.where(diag, a_exp, 0.0), axis=0, keepdims=True)
        dec = jnp.exp(jnp.where(causal, a_exp - a_row, NEG))
        bm_t = jnp.concatenate([_pad_rows(bm, SLOT)] * SSD_HPG, axis=0)
        m = (_dot_nt(cm, bm_t) * dec).astype(BF16)
        xdt = _pad_rows(x * dt_exp, SLOT).astype(BF16)
        ys = []
        for pr in range(SSD_HPG // 2):
            ps = slice(pr * 2 * SSD_P, (pr + 1) * 2 * SSD_P)
            xp = xdt[:, ps]
            xblk = jnp.where(sel, jnp.concatenate([xp, xp], axis=0), 0.0)
            ys.append(jnp.dot(m[:, ps], xblk, preferred_element_type=F32))
        y = jnp.concatenate(ys, axis=1)
        a_last = a_exp[c - 1:c, :]
        wx = x * (jnp.exp(a_last - a_exp) * dt_exp)
        if transposed:
            st = st_ref[g]
            y = y + _dot(cm, st) * jnp.exp(a_exp)
            st_ref[g] = st * jnp.exp(a_last) + _dot_tn(bm, wx)
        else:
            s = s_ref[0, 0, g]
            y = y + _dot_nt(cm, s) * jnp.exp(a_exp)
            upd = _dot_tn(wx, bm)
            sdec = jnp.exp(acum_all[c - 1:c, hs])
            for hh in range(SSD_HPG):
                rs = slice(hh * SSD_P, (hh + 1) * SSD_P)
                s_ref[0, 0, g, rs, :] = s[rs, :] * sdec[:, hh:hh + 1] + upd[rs, :]
        y = y + d_ref[:, gs] * x
        y = y * _silu(z_ref[0, :, gs])
        o_ref[0, :, gs] = (_rms(y) * nw_ref[:, gs]).astype(BF16)

    if transposed:
        @pl.when(pl.program_id(1) == last)
        def _():
            for g in range(SSD_G):
                s_ref[0, 0, g] = jnp.transpose(st_ref[g])


def _ssd(main, small, cw, cb, alog, d_exp, normw, init, prev, *, layer, n_layers, c, valid,
         transposed):
    bsz, L, _ = main.shape
    nc = L // c
    has_init = init is not None
    has_prev = prev is not None
    kern = functools.partial(_ssd_kernel, c=c, valid=valid, has_init=has_init, has_prev=has_prev,
                             transposed=transposed, last=nc - 1)
    bc_blk = 2 * D_INNER // SSD_BC
    in_specs = [
        pl.BlockSpec((1, c, D_INNER), lambda b, n: (b, n, 0)),
        pl.BlockSpec((1, c, D_INNER), lambda b, n: (b, n, 1)),
        pl.BlockSpec((1, c, SSD_BC), lambda b, n: (b, n, bc_blk)),
        pl.BlockSpec((1, c, LANES), lambda b, n: (b, n, 0)),
        pl.BlockSpec((CONV, D_INNER), lambda b, n: (0, 0)),
        pl.BlockSpec((CONV, SSD_BC), lambda b, n: (0, D_INNER // SSD_BC)),
        pl.BlockSpec((1, D_INNER), lambda b, n: (0, 0)),
        pl.BlockSpec((1, SSD_BC), lambda b, n: (0, D_INNER // SSD_BC)),
        pl.BlockSpec((1, SSD_HEADS), lambda b, n: (0, 0)),
        pl.BlockSpec((1, D_INNER), lambda b, n: (0, 0)),
        pl.BlockSpec((1, D_INNER), lambda b, n: (0, 0)),
    ]
    args = [main, main, main, small, cw, cw, cb, cb, alog, d_exp, normw]
    if has_init:
        s0, conv0 = init
        in_specs += [
            pl.BlockSpec((1, 1, SSD_G, SSD_GW, SSD_N), lambda b, n: (layer, b, 0, 0, 0)),
            pl.BlockSpec((1, 1, HIST, D_INNER), lambda b, n: (layer, b, 0, 0)),
            pl.BlockSpec((1, 1, HIST, SSD_BC), lambda b, n: (layer, b, 0, D_INNER // SSD_BC)),
        ]
        args += [s0, conv0, conv0]
    aliases = {}
    if has_prev:
        aliases = {len(args): 1}
        in_specs.append(pl.BlockSpec(memory_space=pl.ANY))
        args.append(prev)
    scratch = [pltpu.VMEM((SUBLANES + c, D_INNER), F32), pltpu.VMEM((SUBLANES + c, SSD_BC), F32)]
    if transposed:
        scratch.append(pltpu.VMEM((SSD_G, SSD_N, SSD_GW), F32))
    return pl.pallas_call(
        kern,
        out_shape=(jax.ShapeDtypeStruct((bsz, L, D_INNER), BF16),
                   jax.ShapeDtypeStruct((n_layers, bsz, SSD_G, SSD_GW, SSD_N), F32)),
        grid=(bsz, nc),
        in_specs=in_specs,
        out_specs=(pl.BlockSpec((1, c, D_INNER), lambda b, n: (b, n, 0)),
                   pl.BlockSpec((1, 1, SSD_G, SSD_GW, SSD_N), lambda b, n: (layer, b, 0, 0, 0))),
        scratch_shapes=scratch,
        input_output_aliases=aliases,
        compiler_params=_cparams(("parallel", "arbitrary")),
        name="ssd_scan",
    )(*args)


def _prep_params(p):
    n_ab = p['w_in_ab'].shape[0]
    n_c = p['w_in_c'].shape[0]
    o_ag = 2 * QK + 2 * WIDTH
    o_m = o_ag + GLA_LOWRANK
    o_if = o_m + 2 * QK + 3 * WIDTH
    ab, cc = [], []
    wt_ab = jnp.swapaxes(p['w_in_ab'], 1, 2)
    wt_c = jnp.swapaxes(p['w_in_c'], 1, 2)
    w_main_ab = _cast_weights_t(wt_ab, o_if - GLA_LOWRANK, shift_from=o_ag // CAST_T_COLS,
                                shift=GLA_LOWRANK)
    w_main_c = _cast_weights_t(wt_c, D_INNER + SSD_CONV_DIM)
    w_out_ab = _cast_weights(p['w_out_ab'], ((0, D_MODEL, 0),), tr=512)
    w_out_c = _cast_weights(p['w_out_c'], ((0, D_MODEL, 0),), tr=512)
    for i in range(n_ab):
        w_small = jnp.zeros((D_MODEL, LANES), F32)
        w_small = w_small.at[:, :GLA_LOWRANK].set(wt_ab[i, o_ag:o_m, :].T)
        w_small = w_small.at[:, SM_I:SM_I + 2 * HEADS].set(wt_ab[i, o_if:, :].T)
        b_small = jnp.zeros((1, LANES), F32)
        b_small = b_small.at[0, SM_I:SM_I + HEADS].set(p['mlstm_b_i'][i])
        b_small = b_small.at[0, SM_F:SM_F + HEADS].set(p['mlstm_b_f'][i])
        wa = p['gla_w_a2'][i]
        wa_hi = wa.astype(BF16)
        wa_lo = (wa - wa_hi.astype(F32)).astype(BF16)
        ab.append(dict(
            norm=p['norm_ab'][i][None], w_main=w_main_ab, w_small=w_small, b_small=b_small,
            wa3=jnp.concatenate([wa_hi, wa_lo, wa_hi], axis=0),
            ba=p['gla_b_a'][i][None], gnorm=p['gla_norm'][i][None],
            cw=p['mlstm_conv_w'][i], cb=p['mlstm_conv_b'][i][None],
            mnorm=p['mlstm_norm'][i][None], w_out=w_out_ab))
    o_dt = D_INNER + SSD_CONV_DIM
    for i in range(n_c):
        w_small = jnp.zeros((D_MODEL, LANES), F32).at[:, :SSD_HEADS].set(wt_c[i, o_dt:, :].T)
        b_small = jnp.zeros((1, LANES), F32).at[0, :SSD_HEADS].set(p['ssd_dt_bias'][i])
        cc.append(dict(
            norm=p['norm_c'][i][None], w_main=w_main_c, w_small=w_small,
            b_small=b_small, cw=p['ssd_conv_w'][i], cb=p['ssd_conv_b'][i][None],
            alog=p['ssd_a_log'][i][None],
            d_exp=jnp.repeat(p['ssd_d'][i], SSD_P)[None],
            normw=p['ssd_norm'][i][None], w_out=w_out_c))
    return ab, cc


def _trunk(x, states, ab, cc, final_w, *, c, c_true, valid, tm, transposed):
    bsz, L, _ = x.shape
    t = bsz * L
    nc = L // c
    n_ab, n_c = len(ab), len(cc)
    xt = x.reshape(t, D_MODEL)
    gla_o = mc_o = mn_o = mm_o = ssm_o = None
    mconv, sconv = [], []
    row0 = (nc - 1) * c + valid - HIST
    for layer in range(n_ab + n_c):
        i = layer // 2
        if layer % 2 == 0:
            P = ab[i]
            main, small = _inproj(xt, P['norm'], P['w_main'], i, P['w_small'], P['b_small'],
                                  tm=tm, tn=1024)
            main = main.reshape(bsz, L, -1)
            small = small.reshape(bsz, L, LANES)
            gates_t = jnp.swapaxes(
                small[:, :, SM_I:SM_I + 2 * HEADS].reshape(bsz, nc, c, 2 * HEADS), 2, 3)
            init = None if states is None else (states['gla'], states['mc'], states['mn'],
                                                states['mm'], states['mconv'])
            prev = None if gla_o is None else (gla_o, mc_o, mn_o, mm_o)
            y, gla_o, mc_o, mn_o, mm_o = _ab_scan(main, small, gates_t, P, init, prev, layer=i,
                                                  n_layers=n_ab, c=c, c_true=c_true, valid=valid)
            mconv.append(main[:, row0:row0 + HIST, 2 * QK + 2 * WIDTH:2 * QK + 2 * WIDTH + 2 * QK])
            y2 = y.reshape(t, D_INNER)
            xt = _outproj(y2, y2, 0, 1, P['w_out'], i, xt, tm=tm, tn=512)
        else:
            P = cc[i]
            main, small = _inproj(xt, P['norm'], P['w_main'], i, P['w_small'], P['b_small'],
                                  tm=tm, tn=1024)
            main = main.reshape(bsz, L, -1)
            small = small.reshape(bsz, L, LANES)
            init = None if states is None else (states['ssm'], states['sconv'])
            y, ssm_o = _ssd(main, small, P['cw'], P['cb'], P['alog'], P['d_exp'], P['normw'],
                            init, ssm_o, layer=i, n_layers=n_c, c=c, valid=valid,
                            transposed=transposed)
            sconv.append(main[:, row0:row0 + HIST, D_INNER:D_INNER + SSD_CONV_DIM])
            y2 = y.reshape(t, D_INNER)
            xt = _outproj(y2, y2, 0, 1, P['w_out'], i, xt, tm=tm, tn=512)
    y = _final_norm(xt, final_w, tm=min(tm, 512)).reshape(bsz, L, D_MODEL)
    return (y, gla_o, mc_o, mn_o[:, :, :, 0, :], mm_o[:, :, :, 0, 0],
            jnp.stack(mconv), ssm_o.reshape(n_c, bsz, SSD_HEADS, SSD_P, SSD_N), jnp.stack(sconv))


def kernel(x_prompt, x_sample, state_gla, state_mlstm_c, state_mlstm_n, state_mlstm_m, state_mlstm_conv, state_ssm, state_ssm_conv, norm_ab, w_in_ab, gla_w_a2, gla_b_a, gla_norm, mlstm_conv_w, mlstm_conv_b, mlstm_b_i, mlstm_b_f, mlstm_norm, w_out_ab, norm_c, w_in_c, ssd_conv_w, ssd_conv_b, ssd_dt_bias, ssd_a_log, ssd_d, ssd_norm, w_out_c, final_norm):
    p = dict(norm_ab=norm_ab, w_in_ab=w_in_ab, gla_w_a2=gla_w_a2, gla_b_a=gla_b_a, gla_norm=gla_norm,
             mlstm_conv_w=mlstm_conv_w, mlstm_conv_b=mlstm_conv_b, mlstm_b_i=mlstm_b_i,
             mlstm_b_f=mlstm_b_f, mlstm_norm=mlstm_norm, w_out_ab=w_out_ab, norm_c=norm_c,
             w_in_c=w_in_c, ssd_conv_w=ssd_conv_w, ssd_conv_b=ssd_conv_b, ssd_dt_bias=ssd_dt_bias,
             ssd_a_log=ssd_a_log, ssd_d=ssd_d, ssd_norm=ssd_norm, w_out_c=w_out_c)
    ab, cc = _prep_params(p)
    fw = final_norm[None]
    n_ab, dec_b = state_gla.shape[0], state_gla.shape[1]
    n_c = state_ssm.shape[0]
    dec_l = x_sample.shape[1]

    yp, gla_p, mc_p, mn_p, mm_p, mcv_p, ssm_p, scv_p = _trunk(
        x_prompt, None, ab, cc, fw, c=CHUNK, c_true=CHUNK, valid=CHUNK, tm=1024, transposed=True)

    xs = jnp.pad(x_sample, ((0, 0), (0, SUBLANES - dec_l), (0, 0)))
    states = dict(
        gla=state_gla, mc=state_mlstm_c,
        mn=state_mlstm_n.reshape(n_ab, dec_b, HEADS, 1, DK),
        mm=state_mlstm_m.reshape(n_ab, dec_b, HEADS, 1, 1),
        mconv=state_mlstm_conv,
        ssm=state_ssm.reshape(n_c, dec_b, SSD_G, SSD_GW, SSD_N),
        sconv=state_ssm_conv)
    ys, gla_s, mc_s, mn_s, mm_s, mcv_s, ssm_s, scv_s = _trunk(
        xs, states, ab, cc, fw, c=SUBLANES, c_true=dec_l, valid=dec_l, tm=1024, transposed=False)
    ys = ys[:, :dec_l]
    return (yp, ys, gla_p, gla_s, mc_p, mc_s, mn_p, mn_s, mm_p, mm_s,
            mcv_p, mcv_s, ssm_p, ssm_s, scv_p, scv_s)
```

```python
import functools

import jax
import jax.numpy as jnp
from jax import lax
from jax.experimental import pallas as pl
from jax.experimental.pallas import tpu as pltpu

F32 = jnp.float32
BF16 = jnp.bfloat16

D_MODEL = 2048
D_INNER = 4096
CHUNK = 64
EPS = 1e-6
NEG = -1e30
HEADS = 4
DK = 256
DV = 512
QK = HEADS * DK
WIDTH = HEADS * DV
GLA_LOWRANK = 16
GLA_GATE_TEMP = 16.0
CONV = 4
HIST = CONV - 1
SM_I = 16
SM_F = 20
SSD_HEADS = 64
SSD_P = 64
SSD_G = 8
SSD_HPG = 8
SSD_N = 128
SSD_GW = SSD_HPG * SSD_P
SSD_BC = 2 * SSD_G * SSD_N
SSD_CONV_DIM = D_INNER + SSD_BC
LANES = 128
SUBLANES = 8
SLOT = 64
SLOT_LOG2 = 6

VMEM_LIMIT = 56 * 1024 * 1024
SAMPLE_SEQS_PER_STEP = 2

def _cparams(sem):
    return pltpu.CompilerParams(dimension_semantics=sem, vmem_limit_bytes=VMEM_LIMIT)


def _dot(a, b):
    return jnp.dot(a.astype(BF16), b.astype(BF16), preferred_element_type=F32)


def _dot_nt(a, b):
    return lax.dot_general(a.astype(BF16), b.astype(BF16), (((1,), (1,)), ((), ())),
                           preferred_element_type=F32)


def _dot_tn(a, b):
    return lax.dot_general(a.astype(BF16), b.astype(BF16), (((0,), (0,)), ((), ())),
                           preferred_element_type=F32)


def _split3(x):
    x1 = x.astype(BF16).astype(F32)
    r = x - x1
    x2 = r.astype(BF16).astype(F32)
    x3 = (r - x2).astype(BF16).astype(F32)
    return x1, x2, x3


def _dot_m01(m01, x):
    return _dot(jnp.concatenate([m01] * 3, axis=1), jnp.concatenate(_split3(x), axis=0))


def _dot_x01(x, m01):
    return _dot(jnp.concatenate(_split3(x), axis=1), jnp.concatenate([m01] * 3, axis=0))


def _sigmoid(x):
    return 1.0 / (1.0 + jnp.exp(-x))


def _silu(x):
    return x * _sigmoid(x)


def _softplus(x):
    return jnp.maximum(x, 0.0) + jnp.log(1.0 + jnp.exp(-jnp.abs(x)))


def _log_sigmoid(x):
    return -_softplus(-x)


def _rms(x):
    return x * lax.rsqrt(jnp.mean(x * x, axis=-1, keepdims=True) + EPS)


def _tri(c):
    row = lax.broadcasted_iota(jnp.int32, (c, c), 0)
    col = lax.broadcasted_iota(jnp.int32, (c, c), 1)
    return col <= row


def _row_valid(c, valid):
    return lax.broadcasted_iota(jnp.int32, (c, 1), 0) < valid


def _col_valid(c, valid):
    return lax.broadcasted_iota(jnp.int32, (1, c), 1) < valid


NORM_ROWS = 256


def _inproj_kernel(x_ref, nw_ref, w_ref, ws_ref, bs_ref, o_ref, os_ref, xn_ref):
    @pl.when(pl.program_id(1) == 0)
    def _():
        ws = ws_ref[...]
        wh = ws.astype(BF16)
        wl = (ws - wh.astype(F32)).astype(BF16)
        wcat = jnp.concatenate([wh, wl], axis=1)
        nr = min(NORM_ROWS, x_ref.shape[0])
        for r in range(x_ref.shape[0] // nr):
            rows = pl.ds(r * nr, nr)
            xn = _rms(x_ref[rows, :]) * nw_ref[...]
            xh = xn.astype(BF16)
            xl = (xn - xh.astype(F32)).astype(BF16)
            xn_ref[rows, :] = xh
            p = jnp.dot(xh, wcat, preferred_element_type=F32)
            os_ref[rows, :] = (p[:, :LANES] + p[:, LANES:]
                               + jnp.dot(xl, wh, preferred_element_type=F32) + bs_ref[...])

    o_ref[...] = jnp.dot(xn_ref[...], w_ref[...], preferred_element_type=F32)


def _inproj(x, norm_w, w_main, layer, w_small, b_small, *, tm, tn):
    t, d = x.shape
    n = w_main.shape[2]
    return pl.pallas_call(
        _inproj_kernel,
        out_shape=(jax.ShapeDtypeStruct((t, n), F32), jax.ShapeDtypeStruct((t, LANES), F32)),
        grid=(t // tm, n // tn),
        in_specs=[
            pl.BlockSpec((tm, d), lambda i, j: (i, 0)),
            pl.BlockSpec((1, d), lambda i, j: (0, 0)),
            pl.BlockSpec((None, d, tn), lambda i, j: (layer, 0, j)),
            pl.BlockSpec((None, d, LANES), lambda i, j: (layer, 0, 0)),
            pl.BlockSpec((1, LANES), lambda i, j: (0, 0)),
        ],
        out_specs=(pl.BlockSpec((tm, tn), lambda i, j: (i, j)),
                   pl.BlockSpec((tm, LANES), lambda i, j: (i, 0))),
        scratch_shapes=[pltpu.VMEM((tm, d), BF16)],
        compiler_params=_cparams(("parallel", "arbitrary")),
        name="inproj",
    )(x, norm_w, w_main, w_small, b_small)


CAST_COLS = 1024


def _cast_kernel(w_ref, o_ref, *, segs):
    for src, width, dst in segs:
        for off in range(0, width, CAST_COLS):
            n = min(CAST_COLS, width - off)
            o_ref[0, :, dst + off:dst + off + n] = w_ref[0, :, src + off:src + off + n].astype(BF16)


def _cast_weights(w, segs, *, tr):
    nl, rows, cols = w.shape
    n_out = sum(width for _, width, _ in segs)
    return pl.pallas_call(
        functools.partial(_cast_kernel, segs=segs),
        out_shape=jax.ShapeDtypeStruct((nl, rows, n_out), BF16),
        grid=(nl, rows // tr),
        in_specs=[pl.BlockSpec((1, tr, cols), lambda l, i: (l, i, 0))],
        out_specs=pl.BlockSpec((1, tr, n_out), lambda l, i: (l, i, 0)),
        compiler_params=_cparams(("parallel", "parallel")),
        name="cast_weights",
    )(w)


CAST_T_COLS = 1024


def _cast_t_kernel(a_ref, b_ref, o_ref, os_ref, *, shift_from, shift, small):
    j = pl.program_id(1)

    @pl.when(j == 0)
    def _():
        os_ref[0] = jnp.zeros(os_ref.shape[1:], F32)

    lane = lax.broadcasted_iota(jnp.int32, (1, LANES), 1)
    for tile, use_b, lo_, hi_ in small:
        @pl.when(j == tile)
        def _(use_b=use_b, lo_=lo_, hi_=hi_):
            src = b_ref if use_b else a_ref
            os_ref[0] = jnp.where((lane >= lo_) & (lane < hi_),
                                  jnp.transpose(src[0, 0:LANES, :]), os_ref[0])

    def emit(s):
        for k in range(CAST_T_COLS // LANES):
            lo = s + k * LANES
            if lo + LANES <= CAST_T_COLS:
                blk = a_ref[0, lo:lo + LANES, :]
            else:
                blk = jnp.concatenate([a_ref[0, lo:CAST_T_COLS, :],
                                       b_ref[0, 0:lo + LANES - CAST_T_COLS, :]], axis=0)
            o_ref[0, :, k * LANES:(k + 1) * LANES] = jnp.transpose(blk).astype(BF16)

    if shift == 0:
        emit(0)
    else:
        @pl.when(pl.program_id(1) < shift_from)
        def _():
            emit(0)

        @pl.when(pl.program_id(1) >= shift_from)
        def _():
            emit(shift)


def _cast_weights_t(wt, n_out, small, *, shift_from=0, shift=0):
    nl, _, d = wt.shape
    per = CAST_T_COLS // LANES
    return pl.pallas_call(
        functools.partial(_cast_t_kernel, shift_from=shift_from, shift=shift, small=small),
        out_shape=(jax.ShapeDtypeStruct((nl, d, n_out), BF16),
                   jax.ShapeDtypeStruct((nl, d, LANES), F32)),
        grid=(nl, n_out // CAST_T_COLS),
        in_specs=[pl.BlockSpec((1, CAST_T_COLS, d), lambda l, j: (l, j, 0)),
                  pl.BlockSpec((1, LANES, d), lambda l, j: (l, (j + 1) * per, 0))],
        out_specs=(pl.BlockSpec((1, d, CAST_T_COLS), lambda l, j: (l, 0, j)),
                   pl.BlockSpec((1, d, LANES), lambda l, j: (l, 0, 0))),
        compiler_params=_cparams(("parallel", "arbitrary")),
        name="cast_weights_t",
    )(wt, wt)


def _outproj_kernel(a_ref, b_ref, wa_ref, wb_ref, x_ref, o_ref):
    o_ref[...] = (x_ref[...]
                  + jnp.dot(a_ref[...], wa_ref[...], preferred_element_type=F32)
                  + jnp.dot(b_ref[...], wb_ref[...], preferred_element_type=F32))


def _outproj(a, b, a_blk, b_blk, w, layer, x, *, tm, tn):
    t, d = x.shape
    half = D_INNER // 2
    return pl.pallas_call(
        _outproj_kernel,
        out_shape=jax.ShapeDtypeStruct((t, d), F32),
        grid=(t // tm, d // tn),
        in_specs=[
            pl.BlockSpec((tm, half), lambda i, j: (i, a_blk)),
            pl.BlockSpec((tm, half), lambda i, j: (i, b_blk)),
            pl.BlockSpec((None, half, tn), lambda i, j: (layer, 0, j)),
            pl.BlockSpec((None, half, tn), lambda i, j: (layer, 1, j)),
            pl.BlockSpec((tm, tn), lambda i, j: (i, j)),
        ],
        out_specs=pl.BlockSpec((tm, tn), lambda i, j: (i, j)),
        compiler_params=_cparams(("parallel", "arbitrary")),
        name="outproj",
    )(a, b, w, w, x)


def _final_norm_kernel(x_ref, w_ref, o_ref):
    o_ref[...] = _rms(x_ref[...]) * w_ref[...]


def _final_norm(x, w, *, tm):
    t, d = x.shape
    return pl.pallas_call(
        _final_norm_kernel,
        out_shape=jax.ShapeDtypeStruct((t, d), F32),
        grid=(t // tm,),
        in_specs=[pl.BlockSpec((tm, d), lambda i: (i, 0)),
                  pl.BlockSpec((1, d), lambda i: (0, 0))],
        out_specs=pl.BlockSpec((tm, d), lambda i: (i, 0)),
        compiler_params=_cparams(("parallel",)),
        name="final_norm",
    )(x, w)


def _causal_conv(buf_ref, x, w_ref, b_ref, c):
    buf_ref[pl.ds(SUBLANES, c), :] = x
    xp = buf_ref[...]
    out = b_ref[...] + x * w_ref[CONV - 1:CONV, :]
    for s in range(1, CONV):
        out = out + pltpu.roll(xp, s, 0)[SUBLANES:, :] * w_ref[CONV - 1 - s:CONV - s, :]
    buf_ref[pl.ds(SUBLANES - HIST, HIST), :] = buf_ref[pl.ds(SUBLANES + c - HIST, HIST), :]
    return out


def _init_hist(buf_ref, hist):
    buf_ref[pl.ds(SUBLANES - HIST, HIST), :] = hist


def _gla_body(q_ref, k_ref, v_ref, z_ref, sm_ref, wa_ref, ba_ref, gn_ref, s0_ref, o_ref, s_ref,
              *, c, c_true, valid):
    @pl.when(pl.program_id(1) == 0)
    def _():
        if s0_ref is not None:
            s_ref[0, 0] = s0_ref[0, 0]
        else:
            s_ref[0, 0] = jnp.zeros((HEADS, DK, DV), F32)

    ag = sm_ref[0][:, :GLA_LOWRANK]
    a1 = ag.astype(BF16).astype(F32)
    a2 = (ag - a1).astype(BF16).astype(F32)
    gpre = _dot(jnp.concatenate([a1, a1, a2], axis=1), wa_ref[...]) + ba_ref[...]
    g = _log_sigmoid(gpre) * (1.0 / GLA_GATE_TEMP)
    rv = _row_valid(c, valid) if valid < c else None
    if rv is not None:
        g = jnp.where(rv, g, 0.0)
    tri = _tri(c)
    g3 = jnp.concatenate(_split3(g), axis=0).astype(BF16)
    tri3 = jnp.concatenate([tri.astype(F32)] * 3, axis=1).astype(BF16)
    b = jnp.dot(tri3, g3, preferred_element_type=F32)
    dcol = jnp.exp(lax.dot_general(g3, jnp.ones((3 * c, LANES), BF16), (((0,), (0,)), ((), ())),
                                   preferred_element_type=F32))

    for h in range(HEADS):
        ks = slice(h * DK, (h + 1) * DK)
        vs = slice(h * DV, (h + 1) * DV)
        q = q_ref[0, :, ks] * (DK ** -0.5)
        k = k_ref[0, :, ks]
        if rv is not None:
            k = jnp.where(rv, k, 0.0)
        v = v_ref[0, :, vs]
        bh = b[:, ks]
        mid = bh[c_true // 2:c_true // 2 + 1, :]
        b_last = bh[c - 1:c, :]
        s = s_ref[0, 0, h]
        o = _dot(q * jnp.exp(bh), s)
        att = _dot_nt(q * jnp.exp(bh - mid), k * jnp.exp(mid - bh))
        o = o + _dot(jnp.where(tri, att, 0.0), v)
        dc = dcol[ks, :]
        s_ref[0, 0, h] = (s * jnp.concatenate([dc] * (DV // LANES), axis=1)
                          + _dot_tn(k * jnp.exp(b_last - bh), v))
        o_ref[0, :, vs] = (_rms(o) * gn_ref[:, vs] * _silu(z_ref[0, :, vs])).astype(BF16)


def _mlstm_body(qk_ref, v_ref, og_ref, z_ref, sm_ref, gt_ref, cw_ref, cb_ref, mn_ref, init_refs,
                o_ref, c_ref, n_ref, m_ref, buf, *, c, valid):
    has_init = init_refs is not None
    if has_init:
        c0_ref, n0_ref, m0_ref, h0_ref = init_refs

    @pl.when(pl.program_id(1) == 0)
    def _():
        if has_init:
            c_ref[0, 0] = c0_ref[0, 0]
            n_ref[0, 0] = n0_ref[0, 0]
            m_ref[0, 0] = jnp.broadcast_to(m0_ref[0, 0], (HEADS, 1, LANES))
            _init_hist(buf, h0_ref[0, 0])
        else:
            c_ref[0, 0] = jnp.zeros((HEADS, DK, DV), F32)
            n_ref[0, 0] = jnp.zeros((HEADS, 1, DK), F32)
            m_ref[0, 0] = jnp.zeros((HEADS, 1, LANES), F32)
            _init_hist(buf, jnp.zeros((HIST, 2 * QK), F32))

    qk = _silu(_causal_conv(buf, qk_ref[0], cw_ref, cb_ref, c))
    sm = sm_ref[0]
    gt = gt_ref[0, 0]
    rv = _row_valid(c, valid) if valid < c else None
    cv = _col_valid(c, valid) if valid < c else None
    tri = _tri(c)
    row = lax.broadcasted_iota(jnp.int32, (c, c), 0)
    col = lax.broadcasted_iota(jnp.int32, (c, c), 1)
    triu = row <= col

    for h in range(HEADS):
        vs = slice(h * DV, (h + 1) * DV)
        q = qk[:, h * DK:(h + 1) * DK]
        k = qk[:, QK + h * DK:QK + (h + 1) * DK] * (DK ** -0.5)
        v = v_ref[0, :, vs]
        ig_c = sm[:, SM_I + h:SM_I + h + 1]
        lf_c = _log_sigmoid(sm[:, SM_F + h:SM_F + h + 1])
        ig_r = gt[h:h + 1, :]
        lf_r = _log_sigmoid(gt[HEADS + h:HEADS + h + 1, :])
        if rv is not None:
            lf_c = jnp.where(rv, lf_c, 0.0)
            lf_r = jnp.where(cv, lf_r, 0.0)
            ig_c = jnp.where(rv, ig_c, NEG)
            ig_r = jnp.where(cv, ig_r, NEG)
        fcum_c = jnp.sum(jnp.where(tri, lf_r, 0.0), axis=1, keepdims=True)
        fcum_r = jnp.sum(jnp.where(triu, lf_c, 0.0), axis=0, keepdims=True)
        m_prev = m_ref[0, 0, h][:, 0:1]
        dlog = jnp.where(tri, fcum_c - fcum_r + ig_r, NEG)
        inter = fcum_c + m_prev
        m_i = jnp.maximum(inter, jnp.max(dlog, axis=1, keepdims=True))
        w_inter = jnp.exp(inter - m_i)
        qkm = _dot_nt(q, k) * jnp.exp(dlog - m_i)
        cm = c_ref[0, 0, h]
        nm = n_ref[0, 0, h]
        num = _dot(qkm, v) + w_inter * _dot(q, cm)
        den = (jnp.sum(qkm, axis=1, keepdims=True)
               + w_inter * jnp.sum(q * nm, axis=1, keepdims=True))
        den = jnp.maximum(jnp.abs(den), jnp.exp(-m_i))
        hh = num / den
        m_new = m_i[c - 1:c, :]
        f_last = fcum_c[c - 1:c, :]
        w_j = jnp.exp(f_last - fcum_c + ig_c - m_new)
        decay = jnp.exp(f_last + m_prev - m_new)
        kw = w_j * k
        c_ref[0, 0, h] = decay * cm + _dot_tn(kw, v)
        n_ref[0, 0, h] = decay * nm + jnp.sum(kw, axis=0, keepdims=True)
        m_ref[0, 0, h] = jnp.broadcast_to(m_new, (1, LANES))
        hm = _sigmoid(og_ref[0, :, vs]) * hh
        o_ref[0, :, WIDTH + h * DV:WIDTH + (h + 1) * DV] = (
            _rms(hm) * mn_ref[:, vs] * _silu(z_ref[0, :, vs])).astype(BF16)


N_AB_IN = 16
N_AB_INIT = 5
N_AB_STATE_OUT = 4


def _ab_kernel(*refs, nb, c, c_true, valid, has_init, has_prev):
    refs = list(refs)
    (q_ref, k_ref, v_ref, z_ref, sm_ref, wa_ref, ba_ref, gn_ref,
     qkm_ref, vm_ref, og_ref, zm_ref, gt_ref, cw_ref, cb_ref, mn_ref) = refs[:N_AB_IN]
    refs = refs[N_AB_IN:]
    init = []
    if has_init:
        init, refs = refs[:N_AB_INIT], refs[N_AB_INIT:]
    if has_prev:
        refs = refs[N_AB_STATE_OUT:]
    o_ref, s_ref, c_ref, n_ref, m_ref, buf = refs
    for bi in range(nb):
        seq = slice(bi, bi + 1)
        row = lambda r: r.at[seq]
        st = lambda r: r.at[:, seq]
        s0_ref = st(init[0]) if has_init else None
        init_refs = [st(r) for r in init[1:]] if has_init else None
        _gla_body(row(q_ref), row(k_ref), row(v_ref), row(z_ref), row(sm_ref), wa_ref, ba_ref,
                  gn_ref, s0_ref, row(o_ref), st(s_ref), c=c, c_true=c_true, valid=valid)
        _mlstm_body(row(qkm_ref), row(vm_ref), row(og_ref), row(zm_ref), row(sm_ref), row(gt_ref),
                    cw_ref, cb_ref, mn_ref, init_refs, row(o_ref), st(c_ref), st(n_ref), st(m_ref),
                    buf.at[bi], c=c, valid=valid)


def _ab_scan(main, small, gates_t, P, init, prev, *, layer, n_layers, c, c_true, valid,
             nb):
    bsz, L, _ = main.shape
    nc = L // c
    has_init = init is not None
    has_prev = prev is not None
    kern = functools.partial(_ab_kernel, nb=nb, c=c, c_true=c_true, valid=valid,
                             has_init=has_init, has_prev=has_prev)
    base = (2 * QK + 2 * WIDTH) // WIDTH
    row = lambda blk: (lambda b, n: (b, n, blk))
    const = lambda b, n: (0, 0)
    state = lambda b, n: (layer, b, 0, 0, 0)
    in_specs = [
        pl.BlockSpec((nb, c, QK), row(0)),
        pl.BlockSpec((nb, c, QK), row(1)),
        pl.BlockSpec((nb, c, WIDTH), row(2 * QK // WIDTH)),
        pl.BlockSpec((nb, c, WIDTH), row(2 * QK // WIDTH + 1)),
        pl.BlockSpec((nb, c, LANES), row(0)),
        pl.BlockSpec((3 * GLA_LOWRANK, QK), const),
        pl.BlockSpec((1, QK), const),
        pl.BlockSpec((1, WIDTH), const),
        pl.BlockSpec((nb, c, 2 * QK), row(base)),
        pl.BlockSpec((nb, c, WIDTH), row(base + 1)),
        pl.BlockSpec((nb, c, WIDTH), row(base + 2)),
        pl.BlockSpec((nb, c, WIDTH), row(base + 3)),
        pl.BlockSpec((nb, 1, 2 * HEADS, c), lambda b, n: (b, n, 0, 0)),
        pl.BlockSpec((CONV, 2 * QK), const),
        pl.BlockSpec((1, 2 * QK), const),
        pl.BlockSpec((1, WIDTH), const),
    ]
    args = [main, main, main, main, small, P['wa3'], P['ba'], P['gnorm'],
            main, main, main, main, gates_t, P['cw'], P['cb'], P['mnorm']]
    assert len(args) == N_AB_IN
    if has_init:
        s0, c0, n0, m0, conv0 = init
        in_specs += [
            pl.BlockSpec((1, nb, HEADS, DK, DV), state),
            pl.BlockSpec((1, nb, HEADS, DK, DV), state),
            pl.BlockSpec((1, nb, HEADS, 1, DK), state),
            pl.BlockSpec((1, nb, HEADS, 1, 1), state),
            pl.BlockSpec((1, nb, HIST, 2 * QK), lambda b, n: (layer, b, 0, 0)),
        ]
        args += [s0, c0, n0, m0, conv0]
    aliases = {}
    if has_prev:
        for i, p in enumerate(prev):
            aliases[len(args)] = 1 + i
            in_specs.append(pl.BlockSpec(memory_space=pl.ANY))
            args.append(p)
    return pl.pallas_call(
        kern,
        out_shape=(jax.ShapeDtypeStruct((bsz, L, 2 * WIDTH), BF16),
                   jax.ShapeDtypeStruct((n_layers, bsz, HEADS, DK, DV), F32),
                   jax.ShapeDtypeStruct((n_layers, bsz, HEADS, DK, DV), F32),
                   jax.ShapeDtypeStruct((n_layers, bsz, HEADS, 1, DK), F32),
                   jax.ShapeDtypeStruct((n_layers, bsz, HEADS, 1, LANES), F32)),
        grid=(bsz // nb, nc),
        in_specs=in_specs,
        out_specs=(pl.BlockSpec((nb, c, 2 * WIDTH), lambda b, n: (b, n, 0)),
                   pl.BlockSpec((1, nb, HEADS, DK, DV), state),
                   pl.BlockSpec((1, nb, HEADS, DK, DV), state),
                   pl.BlockSpec((1, nb, HEADS, 1, DK), state),
                   pl.BlockSpec((1, nb, HEADS, 1, LANES), state)),
        scratch_shapes=[pltpu.VMEM((nb, SUBLANES + c, 2 * QK), F32)],
        input_output_aliases=aliases,
        compiler_params=_cparams(("parallel", "arbitrary")),
        name="ab_scan",
    )(*args)


def _pad_rows(a, rows):
    if a.shape[0] == rows:
        return a
    return jnp.concatenate([a, jnp.zeros((rows - a.shape[0], a.shape[1]), a.dtype)], axis=0)


N_SSD_ROW_IN = 4
N_SSD_IN = 11
N_SSD_INIT = 3


def _ssd_kernel(*refs, nb, has_init, has_prev, **kw):
    refs = list(refs)
    ins, refs = refs[:N_SSD_IN], refs[N_SSD_IN:]
    init = []
    if has_init:
        init, refs = refs[:N_SSD_INIT], refs[N_SSD_INIT:]
    if has_prev:
        refs = refs[1:]
    (o_ref, s_ref), scratch = refs[:2], refs[2:]
    for bi in range(nb):
        seq = slice(bi, bi + 1)
        _ssd_body(*[r.at[seq] for r in ins[:N_SSD_ROW_IN]], *ins[N_SSD_ROW_IN:],
                  [r.at[:, seq] for r in init], o_ref.at[seq], s_ref.at[:, seq],
                  *[r.at[bi] for r in scratch], **kw)


def _ssd_body(z_ref, xp_ref, bcp_ref, sm_ref, cwx_ref, cwbc_ref, cbx_ref, cbbc_ref,
              alog_ref, d_ref, nw_ref, init_refs, o_ref, s_ref, xbuf, bcbuf, st_ref=None,
              *, c, valid, transposed, last):
    has_init = bool(init_refs)
    if has_init:
        s0_ref, hx_ref, hbc_ref = init_refs

    @pl.when(pl.program_id(1) == 0)
    def _():
        if has_init:
            _init_hist(xbuf, hx_ref[0, 0])
            _init_hist(bcbuf, hbc_ref[0, 0])
            if transposed:
                for g in range(SSD_G):
                    st_ref[g] = jnp.transpose(s0_ref[0, 0, g])
            else:
                s_ref[0, 0] = s0_ref[0, 0]
        else:
            _init_hist(xbuf, jnp.zeros((HIST, D_INNER), F32))
            _init_hist(bcbuf, jnp.zeros((HIST, SSD_BC), F32))
            if transposed:
                st_ref[...] = jnp.zeros((SSD_G, SSD_N, SSD_GW), F32)
            else:
                s_ref[0, 0] = jnp.zeros((SSD_G, SSD_GW, SSD_N), F32)

    xall = _silu(_causal_conv(xbuf, xp_ref[0], cwx_ref, cbx_ref, c))
    bcall = _silu(_causal_conv(bcbuf, bcp_ref[0], cwbc_ref, cbbc_ref, c))
    dt_all = _softplus(sm_ref[0][:, :SSD_HEADS])
    if valid < c:
        dt_all = jnp.where(_row_valid(c, valid), dt_all, 0.0)
    a_all = -jnp.exp(alog_ref[...])
    tri = _tri(c).astype(F32)
    acum_all = _dot_m01(tri, dt_all * a_all)

    e = (lax.shift_right_logical(lax.broadcasted_iota(jnp.int32, (SSD_HPG, SSD_GW), 1), SLOT_LOG2)
         == lax.broadcasted_iota(jnp.int32, (SSD_HPG, SSD_GW), 0)).astype(F32)
    rowi = lax.broadcasted_iota(jnp.int32, (c, SSD_GW), 0)
    jpos = jnp.bitwise_and(lax.broadcasted_iota(jnp.int32, (c, SSD_GW), 1), SLOT - 1)
    causal = jpos <= rowi
    diag = jpos == rowi
    sel = (lax.shift_right_logical(lax.broadcasted_iota(jnp.int32, (2 * SLOT, 2 * SSD_P), 0), SLOT_LOG2)
           == lax.shift_right_logical(lax.broadcasted_iota(jnp.int32, (2 * SLOT, 2 * SSD_P), 1), SLOT_LOG2))

    for g in range(SSD_G):
        gs = slice(g * SSD_GW, (g + 1) * SSD_GW)
        hs = slice(g * SSD_HPG, (g + 1) * SSD_HPG)
        x = xall[:, gs]
        bm = bcall[:, g * SSD_N:(g + 1) * SSD_N]
        cm = bcall[:, SSD_G * SSD_N + g * SSD_N:SSD_G * SSD_N + (g + 1) * SSD_N]
        ex = _dot_x01(jnp.concatenate([acum_all[:, hs], dt_all[:, hs]], axis=0), e)
        a_exp = ex[:c]
        dt_exp = ex[c:]
        a_row = jnp.sum(jnp.where(diag, a_exp, 0.0), axis=0, keepdims=True)
        dec = jnp.exp(jnp.where(causal, a_exp - a_row, NEG))
        bm_t = jnp.concatenate([_pad_rows(bm, SLOT)] * SSD_HPG, axis=0)
        m = (_dot_nt(cm, bm_t) * dec).astype(BF16)
        xdt = _pad_rows(x * dt_exp, SLOT).astype(BF16)
        ys = []
        for pr in range(SSD_HPG // 2):
            ps = slice(pr * 2 * SSD_P, (pr + 1) * 2 * SSD_P)
            xp = xdt[:, ps]
            xblk = jnp.where(sel, jnp.concatenate([xp, xp], axis=0), 0.0)
            ys.append(jnp.dot(m[:, ps], xblk, preferred_element_type=F32))
        y = jnp.concatenate(ys, axis=1)
        a_last = a_exp[c - 1:c, :]
        wx = x * (jnp.exp(a_last - a_exp) * dt_exp)
        if transposed:
            st = st_ref[g]
            y = y + _dot(cm, st) * jnp.exp(a_exp)
            st_ref[g] = st * jnp.exp(a_last) + _dot_tn(bm, wx)
        else:
            s = s_ref[0, 0, g]
            y = y + _dot_nt(cm, s) * jnp.exp(a_exp)
            upd = _dot_tn(wx, bm)
            sdec = jnp.exp(acum_all[c - 1:c, hs])
            for hh in range(SSD_HPG):
                rs = slice(hh * SSD_P, (hh + 1) * SSD_P)
                s_ref[0, 0, g, rs, :] = s[rs, :] * sdec[:, hh:hh + 1] + upd[rs, :]
        y = y + d_ref[:, gs] * x
        y = y * _silu(z_ref[0, :, gs])
        o_ref[0, :, gs] = (_rms(y) * nw_ref[:, gs]).astype(BF16)

    if transposed:
        @pl.when(pl.program_id(1) == last)
        def _():
            for g in range(SSD_G):
                s_ref[0, 0, g] = jnp.transpose(st_ref[g])


def _ssd(main, small, cw, cb, alog, d_exp, normw, init, prev, *, layer, n_layers, c, valid,
         transposed, nb):
    bsz, L, _ = main.shape
    nc = L // c
    has_init = init is not None
    has_prev = prev is not None
    kern = functools.partial(_ssd_kernel, nb=nb, c=c, valid=valid, has_init=has_init,
                             has_prev=has_prev, transposed=transposed, last=nc - 1)
    bc_blk = 2 * D_INNER // SSD_BC
    in_specs = [
        pl.BlockSpec((nb, c, D_INNER), lambda b, n: (b, n, 0)),
        pl.BlockSpec((nb, c, D_INNER), lambda b, n: (b, n, 1)),
        pl.BlockSpec((nb, c, SSD_BC), lambda b, n: (b, n, bc_blk)),
        pl.BlockSpec((nb, c, LANES), lambda b, n: (b, n, 0)),
        pl.BlockSpec((CONV, D_INNER), lambda b, n: (0, 0)),
        pl.BlockSpec((CONV, SSD_BC), lambda b, n: (0, D_INNER // SSD_BC)),
        pl.BlockSpec((1, D_INNER), lambda b, n: (0, 0)),
        pl.BlockSpec((1, SSD_BC), lambda b, n: (0, D_INNER // SSD_BC)),
        pl.BlockSpec((1, SSD_HEADS), lambda b, n: (0, 0)),
        pl.BlockSpec((1, D_INNER), lambda b, n: (0, 0)),
        pl.BlockSpec((1, D_INNER), lambda b, n: (0, 0)),
    ]
    args = [main, main, main, small, cw, cw, cb, cb, alog, d_exp, normw]
    if has_init:
        s0, conv0 = init
        in_specs += [
            pl.BlockSpec((1, nb, SSD_G, SSD_GW, SSD_N), lambda b, n: (layer, b, 0, 0, 0)),
            pl.BlockSpec((1, nb, HIST, D_INNER), lambda b, n: (layer, b, 0, 0)),
            pl.BlockSpec((1, nb, HIST, SSD_BC), lambda b, n: (layer, b, 0, D_INNER // SSD_BC)),
        ]
        args += [s0, conv0, conv0]
    aliases = {}
    if has_prev:
        aliases = {len(args): 1}
        in_specs.append(pl.BlockSpec(memory_space=pl.ANY))
        args.append(prev)
    scratch = [pltpu.VMEM((nb, SUBLANES + c, D_INNER), F32),
               pltpu.VMEM((nb, SUBLANES + c, SSD_BC), F32)]
    if transposed:
        scratch.append(pltpu.VMEM((nb, SSD_G, SSD_N, SSD_GW), F32))
    return pl.pallas_call(
        kern,
        out_shape=(jax.ShapeDtypeStruct((bsz, L, D_INNER), BF16),
                   jax.ShapeDtypeStruct((n_layers, bsz, SSD_G, SSD_GW, SSD_N), F32)),
        grid=(bsz // nb, nc),
        in_specs=in_specs,
        out_specs=(pl.BlockSpec((nb, c, D_INNER), lambda b, n: (b, n, 0)),
                   pl.BlockSpec((1, nb, SSD_G, SSD_GW, SSD_N), lambda b, n: (layer, b, 0, 0, 0))),
        scratch_shapes=scratch,
        input_output_aliases=aliases,
        compiler_params=_cparams(("parallel", "arbitrary")),
        name="ssd_scan",
    )(*args)


def _prep_params(p):
    n_ab = p['w_in_ab'].shape[0]
    n_c = p['w_in_c'].shape[0]
    o_ag = 2 * QK + 2 * WIDTH
    o_m = o_ag + GLA_LOWRANK
    o_if = o_m + 2 * QK + 3 * WIDTH
    ab, cc = [], []
    wt_ab = jnp.swapaxes(p['w_in_ab'], 1, 2)
    wt_c = jnp.swapaxes(p['w_in_c'], 1, 2)
    n_main_ab = o_if - GLA_LOWRANK
    o_dt = D_INNER + SSD_CONV_DIM
    w_main_ab, w_small_ab = _cast_weights_t(
        wt_ab, n_main_ab,
        ((o_ag // CAST_T_COLS, False, 0, GLA_LOWRANK),
         (n_main_ab // CAST_T_COLS - 1, True, SM_I, SM_I + 2 * HEADS)),
        shift_from=o_ag // CAST_T_COLS, shift=GLA_LOWRANK)
    w_main_c, w_small_c = _cast_weights_t(
        wt_c, o_dt, ((o_dt // CAST_T_COLS - 1, True, 0, SSD_HEADS),))
    w_out_ab = _cast_weights(p['w_out_ab'], ((0, D_MODEL, 0),), tr=512)
    w_out_c = _cast_weights(p['w_out_c'], ((0, D_MODEL, 0),), tr=512)
    for i in range(n_ab):
        b_small = jnp.zeros((1, LANES), F32)
        b_small = b_small.at[0, SM_I:SM_I + HEADS].set(p['mlstm_b_i'][i])
        b_small = b_small.at[0, SM_F:SM_F + HEADS].set(p['mlstm_b_f'][i])
        wa = p['gla_w_a2'][i]
        wa_hi = wa.astype(BF16)
        wa_lo = (wa - wa_hi.astype(F32)).astype(BF16)
        ab.append(dict(
            norm=p['norm_ab'][i][None], w_main=w_main_ab, w_small=w_small_ab, b_small=b_small,
            wa3=jnp.concatenate([wa_hi, wa_lo, wa_hi], axis=0),
            ba=p['gla_b_a'][i][None], gnorm=p['gla_norm'][i][None],
            cw=p['mlstm_conv_w'][i], cb=p['mlstm_conv_b'][i][None],
            mnorm=p['mlstm_norm'][i][None], w_out=w_out_ab))
    for i in range(n_c):
        b_small = jnp.zeros((1, LANES), F32).at[0, :SSD_HEADS].set(p['ssd_dt_bias'][i])
        cc.append(dict(
            norm=p['norm_c'][i][None], w_main=w_main_c, w_small=w_small_c,
            b_small=b_small, cw=p['ssd_conv_w'][i], cb=p['ssd_conv_b'][i][None],
            alog=p['ssd_a_log'][i][None],
            d_exp=jnp.repeat(p['ssd_d'][i], SSD_P)[None],
            normw=p['ssd_norm'][i][None], w_out=w_out_c))
    return ab, cc


def _trunk(x, states, ab, cc, final_w, *, c, c_true, valid, tm, transposed, nb):
    bsz, L, _ = x.shape
    t = bsz * L
    nc = L // c
    n_ab, n_c = len(ab), len(cc)
    xt = x.reshape(t, D_MODEL)
    gla_o = mc_o = mn_o = mm_o = ssm_o = None
    mconv, sconv = [], []
    row0 = (nc - 1) * c + valid - HIST
    for layer in range(n_ab + n_c):
        i = layer // 2
        if layer % 2 == 0:
            P = ab[i]
            main, small = _inproj(xt, P['norm'], P['w_main'], i, P['w_small'], P['b_small'],
                                  tm=tm, tn=1024)
            main = main.reshape(bsz, L, -1)
            small = small.reshape(bsz, L, LANES)
            gates_t = jnp.swapaxes(
                small[:, :, SM_I:SM_I + 2 * HEADS].reshape(bsz, nc, c, 2 * HEADS), 2, 3)
            init = None if states is None else (states['gla'], states['mc'], states['mn'],
                                                states['mm'], states['mconv'])
            prev = None if gla_o is None else (gla_o, mc_o, mn_o, mm_o)
            y, gla_o, mc_o, mn_o, mm_o = _ab_scan(main, small, gates_t, P, init, prev, layer=i,
                                                  n_layers=n_ab, c=c, c_true=c_true, valid=valid,
                                                  nb=nb)
            mconv.append(main[:, row0:row0 + HIST, 2 * QK + 2 * WIDTH:2 * QK + 2 * WIDTH + 2 * QK])
            y2 = y.reshape(t, D_INNER)
            xt = _outproj(y2, y2, 0, 1, P['w_out'], i, xt, tm=tm, tn=512)
        else:
            P = cc[i]
            main, small = _inproj(xt, P['norm'], P['w_main'], i, P['w_small'], P['b_small'],
                                  tm=tm, tn=1024)
            main = main.reshape(bsz, L, -1)
            small = small.reshape(bsz, L, LANES)
            init = None if states is None else (states['ssm'], states['sconv'])
            y, ssm_o = _ssd(main, small, P['cw'], P['cb'], P['alog'], P['d_exp'], P['normw'],
                            init, ssm_o, layer=i, n_layers=n_c, c=c, valid=valid,
                            transposed=transposed, nb=nb)
            sconv.append(main[:, row0:row0 + HIST, D_INNER:D_INNER + SSD_CONV_DIM])
            y2 = y.reshape(t, D_INNER)
            xt = _outproj(y2, y2, 0, 1, P['w_out'], i, xt, tm=tm, tn=512)
    y = _final_norm(xt, final_w, tm=min(tm, 512)).reshape(bsz, L, D_MODEL)
    return (y, gla_o, mc_o, mn_o[:, :, :, 0, :], mm_o[:, :, :, 0, 0],
            jnp.stack(mconv), ssm_o.reshape(n_c, bsz, SSD_HEADS, SSD_P, SSD_N), jnp.stack(sconv))


def kernel(x_prompt, x_sample, state_gla, state_mlstm_c, state_mlstm_n, state_mlstm_m, state_mlstm_conv, state_ssm, state_ssm_conv, norm_ab, w_in_ab, gla_w_a2, gla_b_a, gla_norm, mlstm_conv_w, mlstm_conv_b, mlstm_b_i, mlstm_b_f, mlstm_norm, w_out_ab, norm_c, w_in_c, ssd_conv_w, ssd_conv_b, ssd_dt_bias, ssd_a_log, ssd_d, ssd_norm, w_out_c, final_norm):
    p = dict(norm_ab=norm_ab, w_in_ab=w_in_ab, gla_w_a2=gla_w_a2, gla_b_a=gla_b_a, gla_norm=gla_norm,
             mlstm_conv_w=mlstm_conv_w, mlstm_conv_b=mlstm_conv_b, mlstm_b_i=mlstm_b_i,
             mlstm_b_f=mlstm_b_f, mlstm_norm=mlstm_norm, w_out_ab=w_out_ab, norm_c=norm_c,
             w_in_c=w_in_c, ssd_conv_w=ssd_conv_w, ssd_conv_b=ssd_conv_b, ssd_dt_bias=ssd_dt_bias,
             ssd_a_log=ssd_a_log, ssd_d=ssd_d, ssd_norm=ssd_norm, w_out_c=w_out_c)
    ab, cc = _prep_params(p)
    fw = final_norm[None]
    n_ab, dec_b = state_gla.shape[0], state_gla.shape[1]
    n_c = state_ssm.shape[0]
    dec_l = x_sample.shape[1]

    yp, gla_p, mc_p, mn_p, mm_p, mcv_p, ssm_p, scv_p = _trunk(
        x_prompt, None, ab, cc, fw, c=CHUNK, c_true=CHUNK, valid=CHUNK, tm=1024, transposed=True,
        nb=1)

    xs = jnp.pad(x_sample, ((0, 0), (0, SUBLANES - dec_l), (0, 0)))
    states = dict(
        gla=state_gla, mc=state_mlstm_c,
        mn=state_mlstm_n.reshape(n_ab, dec_b, HEADS, 1, DK),
        mm=state_mlstm_m.reshape(n_ab, dec_b, HEADS, 1, 1),
        mconv=state_mlstm_conv,
        ssm=state_ssm.reshape(n_c, dec_b, SSD_G, SSD_GW, SSD_N),
        sconv=state_ssm_conv)
    ys, gla_s, mc_s, mn_s, mm_s, mcv_s, ssm_s, scv_s = _trunk(
        xs, states, ab, cc, fw, c=SUBLANES, c_true=dec_l, valid=dec_l, tm=1024, transposed=False,
        nb=SAMPLE_SEQS_PER_STEP)
    ys = ys[:, :dec_l]
    return (yp, ys, gla_p, gla_s, mc_p, mc_s, mn_p, mn_s, mm_p, mm_s,
            mcv_p, mcv_s, ssm_p, ssm_s, scv_p, scv_s)
```

```python
import functools
from typing import NamedTuple

import jax
import jax.numpy as jnp
from jax import lax
from jax.experimental import pallas as pl
from jax.experimental.pallas import tpu as pltpu

F32 = jnp.float32
BF16 = jnp.bfloat16

D_MODEL = 2048
D_INNER = 4096
CHUNK = 64
EPS = 1e-6
NEG = -1e30
HEADS = 4
DK = 256
DV = 512
QK = HEADS * DK
WIDTH = HEADS * DV
GLA_LOWRANK = 16
GLA_GATE_TEMP = 16.0
CONV = 4
HIST = CONV - 1
SM_I = 16
SM_F = 20
SSD_HEADS = 64
SSD_P = 64
SSD_G = 8
SSD_HPG = 8
SSD_N = 128
SSD_GW = SSD_HPG * SSD_P
SSD_BC = 2 * SSD_G * SSD_N
SSD_CONV_DIM = D_INNER + SSD_BC
LANES = 128
SUBLANES = 8
SLOT = 64
SLOT_LOG2 = 6

VMEM_LIMIT = 56 * 1024 * 1024
SAMPLE_SEQS_PER_STEP = 2

def _cparams(sem):
    return pltpu.CompilerParams(dimension_semantics=sem, vmem_limit_bytes=VMEM_LIMIT)


def _dot(a, b):
    return jnp.dot(a.astype(BF16), b.astype(BF16), preferred_element_type=F32)


def _dot_nt(a, b):
    return lax.dot_general(a.astype(BF16), b.astype(BF16), (((1,), (1,)), ((), ())),
                           preferred_element_type=F32)


def _dot_tn(a, b):
    return lax.dot_general(a.astype(BF16), b.astype(BF16), (((0,), (0,)), ((), ())),
                           preferred_element_type=F32)


def _split3(x):
    x1 = x.astype(BF16).astype(F32)
    r = x - x1
    x2 = r.astype(BF16).astype(F32)
    x3 = (r - x2).astype(BF16).astype(F32)
    return x1, x2, x3


def _dot_m01(m01, x):
    return _dot(jnp.concatenate([m01] * 3, axis=1), jnp.concatenate(_split3(x), axis=0))


def _dot_x01(x, m01):
    return _dot(jnp.concatenate(_split3(x), axis=1), jnp.concatenate([m01] * 3, axis=0))


def _sigmoid(x):
    return 1.0 / (1.0 + jnp.exp(-x))


def _silu(x):
    return x * _sigmoid(x)


def _softplus(x):
    return jnp.maximum(x, 0.0) + jnp.log(1.0 + jnp.exp(-jnp.abs(x)))


def _log_sigmoid(x):
    return -_softplus(-x)


def _rms(x):
    return x * lax.rsqrt(jnp.mean(x * x, axis=-1, keepdims=True) + EPS)


def _tri(c):
    row = lax.broadcasted_iota(jnp.int32, (c, c), 0)
    col = lax.broadcasted_iota(jnp.int32, (c, c), 1)
    return col <= row


def _row_valid(c, valid):
    return lax.broadcasted_iota(jnp.int32, (c, 1), 0) < valid


def _col_valid(c, valid):
    return lax.broadcasted_iota(jnp.int32, (1, c), 1) < valid


NORM_ROWS = 256


def _inproj_kernel(x_ref, nw_ref, w_ref, ws_ref, bs_ref, o_ref, os_ref, xn_ref):
    @pl.when(pl.program_id(1) == 0)
    def _():
        ws = ws_ref[...]
        wh = ws.astype(BF16)
        wl = (ws - wh.astype(F32)).astype(BF16)
        wcat = jnp.concatenate([wh, wl], axis=1)
        nr = min(NORM_ROWS, x_ref.shape[0])
        for r in range(x_ref.shape[0] // nr):
            rows = pl.ds(r * nr, nr)
            xn = _rms(x_ref[rows, :]) * nw_ref[...]
            xh = xn.astype(BF16)
            xl = (xn - xh.astype(F32)).astype(BF16)
            xn_ref[rows, :] = xh
            p = jnp.dot(xh, wcat, preferred_element_type=F32)
            os_ref[rows, :] = (p[:, :LANES] + p[:, LANES:]
                               + jnp.dot(xl, wh, preferred_element_type=F32) + bs_ref[...])

    o_ref[...] = jnp.dot(xn_ref[...], w_ref[...], preferred_element_type=F32)


def _inproj(x, norm_w, w_main, layer, w_small, b_small, *, tm, tn):
    t, d = x.shape
    n = w_main.shape[2]
    return pl.pallas_call(
        _inproj_kernel,
        out_shape=(jax.ShapeDtypeStruct((t, n), F32), jax.ShapeDtypeStruct((t, LANES), F32)),
        grid=(t // tm, n // tn),
        in_specs=[
            pl.BlockSpec((tm, d), lambda i, j: (i, 0)),
            pl.BlockSpec((1, d), lambda i, j: (0, 0)),
            pl.BlockSpec((None, d, tn), lambda i, j: (layer, 0, j)),
            pl.BlockSpec((None, d, LANES), lambda i, j: (layer, 0, 0)),
            pl.BlockSpec((1, LANES), lambda i, j: (0, 0)),
        ],
        out_specs=(pl.BlockSpec((tm, tn), lambda i, j: (i, j)),
                   pl.BlockSpec((tm, LANES), lambda i, j: (i, 0))),
        scratch_shapes=[pltpu.VMEM((tm, d), BF16)],
        compiler_params=_cparams(("parallel", "arbitrary")),
        name="inproj",
    )(x, norm_w, w_main, w_small, b_small)


CAST_COLS = 1024


def _cast_kernel(w_ref, o_ref, *, segs):
    for src, width, dst in segs:
        for off in range(0, width, CAST_COLS):
            n = min(CAST_COLS, width - off)
            o_ref[0, :, dst + off:dst + off + n] = w_ref[0, :, src + off:src + off + n].astype(BF16)


def _cast_weights(w, segs, *, tr):
    nl, rows, cols = w.shape
    n_out = sum(width for _, width, _ in segs)
    return pl.pallas_call(
        functools.partial(_cast_kernel, segs=segs),
        out_shape=jax.ShapeDtypeStruct((nl, rows, n_out), BF16),
        grid=(nl, rows // tr),
        in_specs=[pl.BlockSpec((1, tr, cols), lambda l, i: (l, i, 0))],
        out_specs=pl.BlockSpec((1, tr, n_out), lambda l, i: (l, i, 0)),
        compiler_params=_cparams(("parallel", "parallel")),
        name="cast_weights",
    )(w)


CAST_T_COLS = 1024


def _cast_t_kernel(a_ref, b_ref, o_ref, os_ref, *, shift_from, shift, small):
    j = pl.program_id(1)

    @pl.when(j == 0)
    def _():
        os_ref[0] = jnp.zeros(os_ref.shape[1:], F32)

    lane = lax.broadcasted_iota(jnp.int32, (1, LANES), 1)
    for tile, use_b, lo_, hi_ in small:
        @pl.when(j == tile)
        def _(use_b=use_b, lo_=lo_, hi_=hi_):
            src = b_ref if use_b else a_ref
            os_ref[0] = jnp.where((lane >= lo_) & (lane < hi_),
                                  jnp.transpose(src[0, 0:LANES, :]), os_ref[0])

    def emit(s):
        for k in range(CAST_T_COLS // LANES):
            lo = s + k * LANES
            if lo + LANES <= CAST_T_COLS:
                blk = a_ref[0, lo:lo + LANES, :]
            else:
                blk = jnp.concatenate([a_ref[0, lo:CAST_T_COLS, :],
                                       b_ref[0, 0:lo + LANES - CAST_T_COLS, :]], axis=0)
            o_ref[0, :, k * LANES:(k + 1) * LANES] = jnp.transpose(blk).astype(BF16)

    if shift == 0:
        emit(0)
    else:
        @pl.when(pl.program_id(1) < shift_from)
        def _():
            emit(0)

        @pl.when(pl.program_id(1) >= shift_from)
        def _():
            emit(shift)


def _cast_weights_t(wt, n_out, small, *, shift_from=0, shift=0):
    nl, _, d = wt.shape
    per = CAST_T_COLS // LANES
    return pl.pallas_call(
        functools.partial(_cast_t_kernel, shift_from=shift_from, shift=shift, small=small),
        out_shape=(jax.ShapeDtypeStruct((nl, d, n_out), BF16),
                   jax.ShapeDtypeStruct((nl, d, LANES), F32)),
        grid=(nl, n_out // CAST_T_COLS),
        in_specs=[pl.BlockSpec((1, CAST_T_COLS, d), lambda l, j: (l, j, 0)),
                  pl.BlockSpec((1, LANES, d), lambda l, j: (l, (j + 1) * per, 0))],
        out_specs=(pl.BlockSpec((1, d, CAST_T_COLS), lambda l, j: (l, 0, j)),
                   pl.BlockSpec((1, d, LANES), lambda l, j: (l, 0, 0))),
        compiler_params=_cparams(("parallel", "arbitrary")),
        name="cast_weights_t",
    )(wt, wt)


def _outproj_kernel(a_ref, b_ref, wa_ref, wb_ref, x_ref, o_ref):
    o_ref[...] = (x_ref[...]
                  + jnp.dot(a_ref[...], wa_ref[...], preferred_element_type=F32)
                  + jnp.dot(b_ref[...], wb_ref[...], preferred_element_type=F32))


def _outproj(a, b, a_blk, b_blk, w, layer, x, *, tm, tn):
    t, d = x.shape
    half = D_INNER // 2
    return pl.pallas_call(
        _outproj_kernel,
        out_shape=jax.ShapeDtypeStruct((t, d), F32),
        grid=(t // tm, d // tn),
        in_specs=[
            pl.BlockSpec((tm, half), lambda i, j: (i, a_blk)),
            pl.BlockSpec((tm, half), lambda i, j: (i, b_blk)),
            pl.BlockSpec((None, half, tn), lambda i, j: (layer, 0, j)),
            pl.BlockSpec((None, half, tn), lambda i, j: (layer, 1, j)),
            pl.BlockSpec((tm, tn), lambda i, j: (i, j)),
        ],
        out_specs=pl.BlockSpec((tm, tn), lambda i, j: (i, j)),
        compiler_params=_cparams(("parallel", "arbitrary")),
        name="outproj",
    )(a, b, w, w, x)


def _final_norm_kernel(x_ref, w_ref, o_ref):
    o_ref[...] = _rms(x_ref[...]) * w_ref[...]


def _final_norm(x, w, *, tm):
    t, d = x.shape
    return pl.pallas_call(
        _final_norm_kernel,
        out_shape=jax.ShapeDtypeStruct((t, d), F32),
        grid=(t // tm,),
        in_specs=[pl.BlockSpec((tm, d), lambda i: (i, 0)),
                  pl.BlockSpec((1, d), lambda i: (0, 0))],
        out_specs=pl.BlockSpec((tm, d), lambda i: (i, 0)),
        compiler_params=_cparams(("parallel",)),
        name="final_norm",
    )(x, w)


def _causal_conv(buf_ref, x, w_ref, b_ref, c):
    buf_ref[pl.ds(SUBLANES, c), :] = x
    xp = buf_ref[...]
    out = b_ref[...] + x * w_ref[CONV - 1:CONV, :]
    for s in range(1, CONV):
        out = out + pltpu.roll(xp, s, 0)[SUBLANES:, :] * w_ref[CONV - 1 - s:CONV - s, :]
    buf_ref[pl.ds(SUBLANES - HIST, HIST), :] = buf_ref[pl.ds(SUBLANES + c - HIST, HIST), :]
    return out


def _init_hist(buf_ref, hist):
    buf_ref[pl.ds(SUBLANES - HIST, HIST), :] = hist


def _run_init(first, fn):
    if first is None:
        fn()
    else:
        pl.when(first)(fn)


def _gla_body(q_ref, k_ref, v_ref, z_ref, sm_ref, wa_ref, ba_ref, gn_ref, s0_ref, o_ref, s_ref,
              *, phase, first, c, c_true, valid):
    read_given = first is None and s0_ref is not None
    if phase == "init":
        if not read_given:
            def init():
                if s0_ref is not None:
                    s_ref[0, 0] = s0_ref[0, 0]
                else:
                    s_ref[0, 0] = jnp.zeros((HEADS, DK, DV), F32)
            _run_init(first, init)
        return
    src_ref = s0_ref if read_given else s_ref

    ag = sm_ref[0][:, :GLA_LOWRANK]
    a1 = ag.astype(BF16).astype(F32)
    a2 = (ag - a1).astype(BF16).astype(F32)
    gpre = _dot(jnp.concatenate([a1, a1, a2], axis=1), wa_ref[...]) + ba_ref[...]
    g = _log_sigmoid(gpre) * (1.0 / GLA_GATE_TEMP)
    rv = _row_valid(c, valid) if valid < c else None
    if rv is not None:
        g = jnp.where(rv, g, 0.0)
    tri = _tri(c)
    g3 = jnp.concatenate(_split3(g), axis=0).astype(BF16)
    tri3 = jnp.concatenate([tri.astype(F32)] * 3, axis=1).astype(BF16)
    b = jnp.dot(tri3, g3, preferred_element_type=F32)
    dcol = jnp.exp(lax.dot_general(g3, jnp.ones((3 * c, LANES), BF16), (((0,), (0,)), ((), ())),
                                   preferred_element_type=F32))

    for h in range(HEADS):
        ks = slice(h * DK, (h + 1) * DK)
        vs = slice(h * DV, (h + 1) * DV)
        q = q_ref[0, :, ks] * (DK ** -0.5)
        k = k_ref[0, :, ks]
        if rv is not None:
            k = jnp.where(rv, k, 0.0)
        v = v_ref[0, :, vs]
        bh = b[:, ks]
        mid = bh[c_true // 2:c_true // 2 + 1, :]
        b_last = bh[c - 1:c, :]
        s = src_ref[0, 0, h]
        o = _dot(q * jnp.exp(bh), s)
        att = _dot_nt(q * jnp.exp(bh - mid), k * jnp.exp(mid - bh))
        o = o + _dot(jnp.where(tri, att, 0.0), v)
        dc = dcol[ks, :]
        s_ref[0, 0, h] = (s * jnp.concatenate([dc] * (DV // LANES), axis=1)
                          + _dot_tn(k * jnp.exp(b_last - bh), v))
        o_ref[0, :, vs] = (_rms(o) * gn_ref[:, vs] * _silu(z_ref[0, :, vs])).astype(BF16)


def _mlstm_body(qk_ref, v_ref, og_ref, z_ref, sm_ref, gt_ref, cw_ref, cb_ref, mn_ref, init_refs,
                o_ref, c_ref, n_ref, m_ref, buf, *, phase, first, c, valid):
    has_init = init_refs is not None
    if has_init:
        c0_ref, n0_ref, m0_ref, h0_ref = init_refs
    read_given = first is None and has_init
    if phase == "init":
        def init():
            if has_init:
                if not read_given:
                    c_ref[0, 0] = c0_ref[0, 0]
                    n_ref[0, 0] = n0_ref[0, 0]
                    m_ref[0, 0] = jnp.broadcast_to(m0_ref[0, 0], (HEADS, 1, LANES))
                _init_hist(buf, h0_ref[0, 0])
            else:
                c_ref[0, 0] = jnp.zeros((HEADS, DK, DV), F32)
                n_ref[0, 0] = jnp.zeros((HEADS, 1, DK), F32)
                m_ref[0, 0] = jnp.zeros((HEADS, 1, LANES), F32)
                _init_hist(buf, jnp.zeros((HIST, 2 * QK), F32))
        _run_init(first, init)
        return
    c_src, n_src, m_src = (c0_ref, n0_ref, m0_ref) if read_given else (c_ref, n_ref, m_ref)

    qk = _silu(_causal_conv(buf, qk_ref[0], cw_ref, cb_ref, c))
    sm = sm_ref[0]
    gt = gt_ref[0, 0]
    rv = _row_valid(c, valid) if valid < c else None
    cv = _col_valid(c, valid) if valid < c else None
    tri = _tri(c)
    row = lax.broadcasted_iota(jnp.int32, (c, c), 0)
    col = lax.broadcasted_iota(jnp.int32, (c, c), 1)
    triu = row <= col

    for h in range(HEADS):
        vs = slice(h * DV, (h + 1) * DV)
        q = qk[:, h * DK:(h + 1) * DK]
        k = qk[:, QK + h * DK:QK + (h + 1) * DK] * (DK ** -0.5)
        v = v_ref[0, :, vs]
        ig_c = sm[:, SM_I + h:SM_I + h + 1]
        lf_c = _log_sigmoid(sm[:, SM_F + h:SM_F + h + 1])
        ig_r = gt[h:h + 1, :]
        lf_r = _log_sigmoid(gt[HEADS + h:HEADS + h + 1, :])
        if rv is not None:
            lf_c = jnp.where(rv, lf_c, 0.0)
            lf_r = jnp.where(cv, lf_r, 0.0)
            ig_c = jnp.where(rv, ig_c, NEG)
            ig_r = jnp.where(cv, ig_r, NEG)
        fcum_c = jnp.sum(jnp.where(tri, lf_r, 0.0), axis=1, keepdims=True)
        fcum_r = jnp.sum(jnp.where(triu, lf_c, 0.0), axis=0, keepdims=True)
        m_prev = m_src[0, 0, h][:, 0:1]
        dlog = jnp.where(tri, fcum_c - fcum_r + ig_r, NEG)
        inter = fcum_c + m_prev
        m_i = jnp.maximum(inter, jnp.max(dlog, axis=1, keepdims=True))
        w_inter = jnp.exp(inter - m_i)
        qkm = _dot_nt(q, k) * jnp.exp(dlog - m_i)
        cm = c_src[0, 0, h]
        nm = n_src[0, 0, h]
        num = _dot(qkm, v) + w_inter * _dot(q, cm)
        den = (jnp.sum(qkm, axis=1, keepdims=True)
               + w_inter * jnp.sum(q * nm, axis=1, keepdims=True))
        den = jnp.maximum(jnp.abs(den), jnp.exp(-m_i))
        hh = num / den
        m_new = m_i[c - 1:c, :]
        f_last = fcum_c[c - 1:c, :]
        w_j = jnp.exp(f_last - fcum_c + ig_c - m_new)
        decay = jnp.exp(f_last + m_prev - m_new)
        kw = w_j * k
        c_ref[0, 0, h] = decay * cm + _dot_tn(kw, v)
        n_ref[0, 0, h] = decay * nm + jnp.sum(kw, axis=0, keepdims=True)
        m_ref[0, 0, h] = jnp.broadcast_to(m_new, (1, LANES))
        hm = _sigmoid(og_ref[0, :, vs]) * hh
        o_ref[0, :, WIDTH + h * DV:WIDTH + (h + 1) * DV] = (
            _rms(hm) * mn_ref[:, vs] * _silu(z_ref[0, :, vs])).astype(BF16)


N_AB_W = 6
N_AB_ROW = 10
N_AB_INIT = 5
N_AB_STATE_OUT = 4


class _Group(NamedTuple):
    nb: int
    c: int
    c_true: int
    valid: int
    chunked: bool
    has_init: bool
    has_prev: bool


def _ab_kernel(*refs, groups):
    refs = list(refs)
    wa_ref, ba_ref, gn_ref, cw_ref, cb_ref, mn_ref = refs[:N_AB_W]
    refs = refs[N_AB_W:]
    ins = []
    for g in groups:
        rows, refs = refs[:N_AB_ROW], refs[N_AB_ROW:]
        init = []
        if g.has_init:
            init, refs = refs[:N_AB_INIT], refs[N_AB_INIT:]
        if g.has_prev:
            refs = refs[N_AB_STATE_OUT:]
        ins.append((rows, init))
    outs = []
    for g in groups:
        outs.append(refs[:1 + N_AB_STATE_OUT])
        refs = refs[1 + N_AB_STATE_OUT:]
    for phase in ("init", "compute"):
        for g, (rows, init), out, buf in zip(groups, ins, outs, refs):
            q_ref, k_ref, v_ref, z_ref, sm_ref, qkm_ref, vm_ref, og_ref, zm_ref, gt_ref = rows
            o_ref, s_ref, c_ref, n_ref, m_ref = out
            first = pl.program_id(1) == 0 if g.chunked else None
            for bi in range(g.nb):
                seq = slice(bi, bi + 1)
                row = lambda r: r.at[seq]
                st = lambda r: r.at[:, seq]
                s0_ref = st(init[0]) if g.has_init else None
                init_refs = [st(r) for r in init[1:]] if g.has_init else None
                _gla_body(row(q_ref), row(k_ref), row(v_ref), row(z_ref), row(sm_ref), wa_ref,
                          ba_ref, gn_ref, s0_ref, row(o_ref), st(s_ref), phase=phase, first=first,
                          c=g.c, c_true=g.c_true, valid=g.valid)
                _mlstm_body(row(qkm_ref), row(vm_ref), row(og_ref), row(zm_ref), row(sm_ref),
                            row(gt_ref), cw_ref, cb_ref, mn_ref, init_refs, row(o_ref), st(c_ref),
                            st(n_ref), st(m_ref), buf.at[bi], phase=phase, first=first, c=g.c,
                            valid=g.valid)


def _ab_scan(data, P, *, layer, n_layers):
    d0 = data[0]
    grid = (d0['main'].shape[0] // d0['nb'], d0['main'].shape[1] // d0['c'])
    base = (2 * QK + 2 * WIDTH) // WIDTH
    const = lambda b, n: (0, 0)
    in_specs = [
        pl.BlockSpec((3 * GLA_LOWRANK, QK), const),
        pl.BlockSpec((1, QK), const),
        pl.BlockSpec((1, WIDTH), const),
        pl.BlockSpec((CONV, 2 * QK), const),
        pl.BlockSpec((1, 2 * QK), const),
        pl.BlockSpec((1, WIDTH), const),
    ]
    args = [P['wa3'], P['ba'], P['gnorm'], P['cw'], P['cb'], P['mnorm']]
    groups, out_shape, out_specs, scratch, aliases = [], [], [], [], {}
    n_out = 0
    for gi, d in enumerate(data):
        main, nb, c = d['main'], d['nb'], d['c']
        bsz, L, _ = main.shape
        chunked = gi == 0
        if chunked:
            seq = lambda b, n: b
            chunk = lambda b, n: n
        else:
            assert L == c and bsz // nb == grid[0] * grid[1]
            seq = lambda b, n: b * grid[1] + n
            chunk = lambda b, n: 0
        row = lambda blk, seq=seq, chunk=chunk: (lambda b, n: (seq(b, n), chunk(b, n), blk))
        state = lambda b, n, seq=seq: (layer, seq(b, n), 0, 0, 0)
        in_specs += [
            pl.BlockSpec((nb, c, QK), row(0)),
            pl.BlockSpec((nb, c, QK), row(1)),
            pl.BlockSpec((nb, c, WIDTH), row(2 * QK // WIDTH)),
            pl.BlockSpec((nb, c, WIDTH), row(2 * QK // WIDTH + 1)),
            pl.BlockSpec((nb, c, LANES), row(0)),
            pl.BlockSpec((nb, c, 2 * QK), row(base)),
            pl.BlockSpec((nb, c, WIDTH), row(base + 1)),
            pl.BlockSpec((nb, c, WIDTH), row(base + 2)),
            pl.BlockSpec((nb, c, WIDTH), row(base + 3)),
            pl.BlockSpec((nb, 1, 2 * HEADS, c),
                         lambda b, n, seq=seq, chunk=chunk: (seq(b, n), chunk(b, n), 0, 0)),
        ]
        args += [main, main, main, main, d['small'], main, main, main, main, d['gates_t']]
        has_init = d['init'] is not None
        has_prev = d['prev'] is not None
        if has_init:
            in_specs += [
                pl.BlockSpec((1, nb, HEADS, DK, DV), state),
                pl.BlockSpec((1, nb, HEADS, DK, DV), state),
                pl.BlockSpec((1, nb, HEADS, 1, DK), state),
                pl.BlockSpec((1, nb, HEADS, 1, 1), state),
                pl.BlockSpec((1, nb, HIST, 2 * QK), lambda b, n, seq=seq: (layer, seq(b, n), 0, 0)),
            ]
            args += list(d['init'])
        if has_prev:
            for i, p in enumerate(d['prev']):
                aliases[len(args)] = n_out + 1 + i
                in_specs.append(pl.BlockSpec(memory_space=pl.ANY))
                args.append(p)
        out_shape += [jax.ShapeDtypeStruct((bsz, L, 2 * WIDTH), BF16),
                      jax.ShapeDtypeStruct((n_layers, bsz, HEADS, DK, DV), F32),
                      jax.ShapeDtypeStruct((n_layers, bsz, HEADS, DK, DV), F32),
                      jax.ShapeDtypeStruct((n_layers, bsz, HEADS, 1, DK), F32),
                      jax.ShapeDtypeStruct((n_layers, bsz, HEADS, 1, LANES), F32)]
        out_specs += [pl.BlockSpec((nb, c, 2 * WIDTH), row(0)),
                      pl.BlockSpec((1, nb, HEADS, DK, DV), state),
                      pl.BlockSpec((1, nb, HEADS, DK, DV), state),
                      pl.BlockSpec((1, nb, HEADS, 1, DK), state),
                      pl.BlockSpec((1, nb, HEADS, 1, LANES), state)]
        n_out += 1 + N_AB_STATE_OUT
        scratch.append(pltpu.VMEM((nb, SUBLANES + c, 2 * QK), F32))
        groups.append(_Group(nb=nb, c=c, c_true=d['c_true'], valid=d['valid'], chunked=chunked,
                             has_init=has_init, has_prev=has_prev))
    res = pl.pallas_call(
        functools.partial(_ab_kernel, groups=tuple(groups)),
        out_shape=tuple(out_shape),
        grid=grid,
        in_specs=in_specs,
        out_specs=tuple(out_specs),
        scratch_shapes=scratch,
        input_output_aliases=aliases,
        compiler_params=_cparams(("parallel", "arbitrary")),
        name="ab_scan",
    )(*args)
    per = 1 + N_AB_STATE_OUT
    return [res[i * per:(i + 1) * per] for i in range(len(data))]


def _pad_rows(a, rows):
    if a.shape[0] == rows:
        return a
    return jnp.concatenate([a, jnp.zeros((rows - a.shape[0], a.shape[1]), a.dtype)], axis=0)


N_SSD_ROW_IN = 4
N_SSD_IN = 11
N_SSD_INIT = 3


def _ssd_kernel(*refs, nb, one_chunk, has_init, has_prev, **kw):
    refs = list(refs)
    ins, refs = refs[:N_SSD_IN], refs[N_SSD_IN:]
    init = []
    if has_init:
        init, refs = refs[:N_SSD_INIT], refs[N_SSD_INIT:]
    if has_prev:
        refs = refs[1:]
    (o_ref, s_ref), scratch = refs[:2], refs[2:]
    first = None if one_chunk else pl.program_id(1) == 0
    for phase in ("init", "compute"):
        for bi in range(nb):
            seq = slice(bi, bi + 1)
            _ssd_body(*[r.at[seq] for r in ins[:N_SSD_ROW_IN]], *ins[N_SSD_ROW_IN:],
                      [r.at[:, seq] for r in init], o_ref.at[seq], s_ref.at[:, seq],
                      *[r.at[bi] for r in scratch], phase=phase, first=first, **kw)


def _ssd_body(z_ref, xp_ref, bcp_ref, sm_ref, cwx_ref, cwbc_ref, cbx_ref, cbbc_ref,
              alog_ref, d_ref, nw_ref, init_refs, o_ref, s_ref, xbuf, bcbuf, st_ref=None,
              *, phase, first, c, valid, transposed, last):
    has_init = bool(init_refs)
    if has_init:
        s0_ref, hx_ref, hbc_ref = init_refs
    read_given = first is None and has_init and not transposed
    if phase == "init":
        def init():
            if has_init:
                _init_hist(xbuf, hx_ref[0, 0])
                _init_hist(bcbuf, hbc_ref[0, 0])
                if transposed:
                    for g in range(SSD_G):
                        st_ref[g] = jnp.transpose(s0_ref[0, 0, g])
                elif not read_given:
                    s_ref[0, 0] = s0_ref[0, 0]
            else:
                _init_hist(xbuf, jnp.zeros((HIST, D_INNER), F32))
                _init_hist(bcbuf, jnp.zeros((HIST, SSD_BC), F32))
                if transposed:
                    st_ref[...] = jnp.zeros((SSD_G, SSD_N, SSD_GW), F32)
                else:
                    s_ref[0, 0] = jnp.zeros((SSD_G, SSD_GW, SSD_N), F32)
        _run_init(first, init)
        return
    src_ref = s0_ref if read_given else s_ref

    xall = _silu(_causal_conv(xbuf, xp_ref[0], cwx_ref, cbx_ref, c))
    bcall = _silu(_causal_conv(bcbuf, bcp_ref[0], cwbc_ref, cbbc_ref, c))
    dt_all = _softplus(sm_ref[0][:, :SSD_HEADS])
    if valid < c:
        dt_all = jnp.where(_row_valid(c, valid), dt_all, 0.0)
    a_all = -jnp.exp(alog_ref[...])
    tri = _tri(c).astype(F32)
    acum_all = _dot_m01(tri, dt_all * a_all)

    slot = min(c, SLOT)
    slot_log2 = slot.bit_length() - 1
    hw = SSD_HPG * slot
    per_mm = min(SSD_HPG, max(1, LANES // slot))
    iota = lambda shape, axis: lax.broadcasted_iota(jnp.int32, shape, axis)
    e_p = (lax.shift_right_logical(iota((SSD_HPG, SSD_GW), 1), SLOT_LOG2)
           == iota((SSD_HPG, SSD_GW), 0)).astype(F32)
    if slot == SSD_P:
        e = e_p
    else:
        e_j = (lax.shift_right_logical(iota((SSD_HPG, hw), 1), slot_log2)
               == iota((SSD_HPG, hw), 0)).astype(F32)
        e = jnp.concatenate([e_p, e_j], axis=1)
    jpos = jnp.bitwise_and(iota((c, hw), 1), slot - 1)
    causal = jpos <= iota((c, hw), 0)
    diag = jpos == iota((c, hw), 0)
    sel = (lax.shift_right_logical(iota((per_mm * slot, per_mm * SSD_P), 0), slot_log2)
           == lax.shift_right_logical(iota((per_mm * slot, per_mm * SSD_P), 1), SLOT_LOG2))
    tile_dtype = BF16 if slot % (2 * SUBLANES) == 0 else F32

    for g in range(SSD_G):
        gs = slice(g * SSD_GW, (g + 1) * SSD_GW)
        hs = slice(g * SSD_HPG, (g + 1) * SSD_HPG)
        x = xall[:, gs]
        bm = bcall[:, g * SSD_N:(g + 1) * SSD_N]
        cm = bcall[:, SSD_G * SSD_N + g * SSD_N:SSD_G * SSD_N + (g + 1) * SSD_N]
        ex = _dot_x01(jnp.concatenate([acum_all[:, hs], dt_all[:, hs]], axis=0), e)
        a_exp = ex[:c, :SSD_GW]
        dt_exp = ex[c:, :SSD_GW]
        a_j = a_exp if slot == SSD_P else ex[:c, SSD_GW:]
        a_row = jnp.sum(jnp.where(diag, a_j, 0.0), axis=0, keepdims=True)
        dec = jnp.exp(jnp.where(causal, a_j - a_row, NEG))
        bm_t = jnp.concatenate([_pad_rows(bm, slot)] * SSD_HPG, axis=0)
        m = (_dot_nt(cm, bm_t) * dec).astype(BF16)
        xdt = _pad_rows(x * dt_exp, slot).astype(tile_dtype)
        ys = []
        for pr in range(SSD_HPG // per_mm):
            xp = xdt[:, pr * per_mm * SSD_P:(pr + 1) * per_mm * SSD_P]
            xblk = jnp.where(sel, jnp.concatenate([xp] * per_mm, axis=0), 0.0).astype(BF16)
            ys.append(jnp.dot(m[:, pr * per_mm * slot:(pr + 1) * per_mm * slot], xblk,
                              preferred_element_type=F32))
        y = ys[0] if len(ys) == 1 else jnp.concatenate(ys, axis=1)
        a_last = a_exp[c - 1:c, :]
        wx = x * (jnp.exp(a_last - a_exp) * dt_exp)
        if transposed:
            st = st_ref[g]
            y = y + _dot(cm, st) * jnp.exp(a_exp)
            st_ref[g] = st * jnp.exp(a_last) + _dot_tn(bm, wx)
        else:
            s = src_ref[0, 0, g]
            y = y + _dot_nt(cm, s) * jnp.exp(a_exp)
            upd = _dot_tn(wx, bm)
            sdec = jnp.exp(acum_all[c - 1:c, hs])
            for hh in range(SSD_HPG):
                rs = slice(hh * SSD_P, (hh + 1) * SSD_P)
                s_ref[0, 0, g, rs, :] = s[rs, :] * sdec[:, hh:hh + 1] + upd[rs, :]
        y = y + d_ref[:, gs] * x
        y = y * _silu(z_ref[0, :, gs])
        o_ref[0, :, gs] = (_rms(y) * nw_ref[:, gs]).astype(BF16)

    if transposed:
        @pl.when(pl.program_id(1) == last)
        def _():
            for g in range(SSD_G):
                s_ref[0, 0, g] = jnp.transpose(st_ref[g])


def _ssd(main, small, cw, cb, alog, d_exp, normw, init, prev, *, layer, n_layers, c, valid,
         transposed, nb):
    bsz, L, _ = main.shape
    nc = L // c
    has_init = init is not None
    has_prev = prev is not None
    kern = functools.partial(_ssd_kernel, nb=nb, one_chunk=nc == 1, c=c, valid=valid,
                             has_init=has_init, has_prev=has_prev, transposed=transposed,
                             last=nc - 1)
    bc_blk = 2 * D_INNER // SSD_BC
    in_specs = [
        pl.BlockSpec((nb, c, D_INNER), lambda b, n: (b, n, 0)),
        pl.BlockSpec((nb, c, D_INNER), lambda b, n: (b, n, 1)),
        pl.BlockSpec((nb, c, SSD_BC), lambda b, n: (b, n, bc_blk)),
        pl.BlockSpec((nb, c, LANES), lambda b, n: (b, n, 0)),
        pl.BlockSpec((CONV, D_INNER), lambda b, n: (0, 0)),
        pl.BlockSpec((CONV, SSD_BC), lambda b, n: (0, D_INNER // SSD_BC)),
        pl.BlockSpec((1, D_INNER), lambda b, n: (0, 0)),
        pl.BlockSpec((1, SSD_BC), lambda b, n: (0, D_INNER // SSD_BC)),
        pl.BlockSpec((1, SSD_HEADS), lambda b, n: (0, 0)),
        pl.BlockSpec((1, D_INNER), lambda b, n: (0, 0)),
        pl.BlockSpec((1, D_INNER), lambda b, n: (0, 0)),
    ]
    args = [main, main, main, small, cw, cw, cb, cb, alog, d_exp, normw]
    if has_init:
        s0, conv0 = init
        in_specs += [
            pl.BlockSpec((1, nb, SSD_G, SSD_GW, SSD_N), lambda b, n: (layer, b, 0, 0, 0)),
            pl.BlockSpec((1, nb, HIST, D_INNER), lambda b, n: (layer, b, 0, 0)),
            pl.BlockSpec((1, nb, HIST, SSD_BC), lambda b, n: (layer, b, 0, D_INNER // SSD_BC)),
        ]
        args += [s0, conv0, conv0]
    aliases = {}
    if has_prev:
        aliases = {len(args): 1}
        in_specs.append(pl.BlockSpec(memory_space=pl.ANY))
        args.append(prev)
    scratch = [pltpu.VMEM((nb, SUBLANES + c, D_INNER), F32),
               pltpu.VMEM((nb, SUBLANES + c, SSD_BC), F32)]
    if transposed:
        scratch.append(pltpu.VMEM((nb, SSD_G, SSD_N, SSD_GW), F32))
    return pl.pallas_call(
        kern,
        out_shape=(jax.ShapeDtypeStruct((bsz, L, D_INNER), BF16),
                   jax.ShapeDtypeStruct((n_layers, bsz, SSD_G, SSD_GW, SSD_N), F32)),
        grid=(bsz // nb, nc),
        in_specs=in_specs,
        out_specs=(pl.BlockSpec((nb, c, D_INNER), lambda b, n: (b, n, 0)),
                   pl.BlockSpec((1, nb, SSD_G, SSD_GW, SSD_N), lambda b, n: (layer, b, 0, 0, 0))),
        scratch_shapes=scratch,
        input_output_aliases=aliases,
        compiler_params=_cparams(("parallel", "arbitrary")),
        name="ssd_scan",
    )(*args)


def _prep_params(p):
    n_ab = p['w_in_ab'].shape[0]
    n_c = p['w_in_c'].shape[0]
    o_ag = 2 * QK + 2 * WIDTH
    o_m = o_ag + GLA_LOWRANK
    o_if = o_m + 2 * QK + 3 * WIDTH
    ab, cc = [], []
    wt_ab = jnp.swapaxes(p['w_in_ab'], 1, 2)
    wt_c = jnp.swapaxes(p['w_in_c'], 1, 2)
    n_main_ab = o_if - GLA_LOWRANK
    o_dt = D_INNER + SSD_CONV_DIM
    w_main_ab, w_small_ab = _cast_weights_t(
        wt_ab, n_main_ab,
        ((o_ag // CAST_T_COLS, False, 0, GLA_LOWRANK),
         (n_main_ab // CAST_T_COLS - 1, True, SM_I, SM_I + 2 * HEADS)),
        shift_from=o_ag // CAST_T_COLS, shift=GLA_LOWRANK)
    w_main_c, w_small_c = _cast_weights_t(
        wt_c, o_dt, ((o_dt // CAST_T_COLS - 1, True, 0, SSD_HEADS),))
    w_out_ab = _cast_weights(p['w_out_ab'], ((0, D_MODEL, 0),), tr=512)
    w_out_c = _cast_weights(p['w_out_c'], ((0, D_MODEL, 0),), tr=512)
    for i in range(n_ab):
        b_small = jnp.zeros((1, LANES), F32)
        b_small = b_small.at[0, SM_I:SM_I + HEADS].set(p['mlstm_b_i'][i])
        b_small = b_small.at[0, SM_F:SM_F + HEADS].set(p['mlstm_b_f'][i])
        wa = p['gla_w_a2'][i]
        wa_hi = wa.astype(BF16)
        wa_lo = (wa - wa_hi.astype(F32)).astype(BF16)
        ab.append(dict(
            norm=p['norm_ab'][i][None], w_main=w_main_ab, w_small=w_small_ab, b_small=b_small,
            wa3=jnp.concatenate([wa_hi, wa_lo, wa_hi], axis=0),
            ba=p['gla_b_a'][i][None], gnorm=p['gla_norm'][i][None],
            cw=p['mlstm_conv_w'][i], cb=p['mlstm_conv_b'][i][None],
            mnorm=p['mlstm_norm'][i][None], w_out=w_out_ab))
    for i in range(n_c):
        b_small = jnp.zeros((1, LANES), F32).at[0, :SSD_HEADS].set(p['ssd_dt_bias'][i])
        cc.append(dict(
            norm=p['norm_c'][i][None], w_main=w_main_c, w_small=w_small_c,
            b_small=b_small, cw=p['ssd_conv_w'][i], cb=p['ssd_conv_b'][i][None],
            alog=p['ssd_a_log'][i][None],
            d_exp=jnp.repeat(p['ssd_d'][i], SSD_P)[None],
            normw=p['ssd_norm'][i][None], w_out=w_out_c))
    return ab, cc


def _trunk(xs, cfgs, states, ab, cc, final_w, *, tm):
    n_ab, n_c = len(ab), len(cc)
    ng = len(xs)
    dims = [x.shape[:2] for x in xs]
    xt = [x.reshape(-1, D_MODEL) for x in xs]
    ab_o = [None] * ng
    ssm_o = [None] * ng
    mconv = [[] for _ in xs]
    sconv = [[] for _ in xs]
    row0 = [L - cfg['c'] + cfg['valid'] - HIST for (_, L), cfg in zip(dims, cfgs)]
    for layer in range(n_ab + n_c):
        i = layer // 2
        P = ab[i] if layer % 2 == 0 else cc[i]
        mains, smalls = [], []
        for g in range(ng):
            main, small = _inproj(xt[g], P['norm'], P['w_main'], i, P['w_small'], P['b_small'],
                                  tm=min(tm, xt[g].shape[0]), tn=1024)
            mains.append(main.reshape(dims[g] + (-1,)))
            smalls.append(small.reshape(dims[g] + (LANES,)))
        if layer % 2 == 0:
            data = []
            for g in range(ng):
                (bsz, L), cfg, st = dims[g], cfgs[g], states[g]
                c = cfg['c']
                gates_t = jnp.swapaxes(smalls[g][:, :, SM_I:SM_I + 2 * HEADS].reshape(
                    bsz, L // c, c, 2 * HEADS), 2, 3)
                init = None if st is None else (st['gla'], st['mc'], st['mn'], st['mm'], st['mconv'])
                prev = None if ab_o[g] is None else ab_o[g][1:]
                data.append(dict(main=mains[g], small=smalls[g], gates_t=gates_t, init=init,
                                 prev=prev, nb=cfg['nb_ab'], c=c, c_true=cfg['c_true'],
                                 valid=cfg['valid']))
                mconv[g].append(mains[g][:, row0[g]:row0[g] + HIST,
                                         2 * QK + 2 * WIDTH:2 * QK + 2 * WIDTH + 2 * QK])
            ab_o = _ab_scan(data, P, layer=i, n_layers=n_ab)
            ys = [o[0] for o in ab_o]
        else:
            ys = []
            for g in range(ng):
                cfg, st = cfgs[g], states[g]
                init = None if st is None else (st['ssm'], st['sconv'])
                y, ssm_o[g] = _ssd(mains[g], smalls[g], P['cw'], P['cb'], P['alog'], P['d_exp'],
                                   P['normw'], init, ssm_o[g], layer=i, n_layers=n_c, c=cfg['c'],
                                   valid=cfg['valid'], transposed=cfg['transposed'],
                                   nb=cfg['nb_ssd'])
                ys.append(y)
                sconv[g].append(mains[g][:, row0[g]:row0[g] + HIST, D_INNER:D_INNER + SSD_CONV_DIM])
        for g in range(ng):
            y2 = ys[g].reshape(-1, D_INNER)
            xt[g] = _outproj(y2, y2, 0, 1, P['w_out'], i, xt[g], tm=min(tm, xt[g].shape[0]),
                             tn=512)
    outs = []
    for g in range(ng):
        bsz, L = dims[g]
        y = _final_norm(xt[g], final_w, tm=min(tm, 512, xt[g].shape[0])).reshape(bsz, L, D_MODEL)
        _, gla_o, mc_o, mn_o, mm_o = ab_o[g]
        outs.append((y, gla_o, mc_o, mn_o[:, :, :, 0, :], mm_o[:, :, :, 0, 0], jnp.stack(mconv[g]),
                     ssm_o[g].reshape(n_c, bsz, SSD_HEADS, SSD_P, SSD_N), jnp.stack(sconv[g])))
    return outs


def kernel(x_prompt, x_sample, state_gla, state_mlstm_c, state_mlstm_n, state_mlstm_m, state_mlstm_conv, state_ssm, state_ssm_conv, norm_ab, w_in_ab, gla_w_a2, gla_b_a, gla_norm, mlstm_conv_w, mlstm_conv_b, mlstm_b_i, mlstm_b_f, mlstm_norm, w_out_ab, norm_c, w_in_c, ssd_conv_w, ssd_conv_b, ssd_dt_bias, ssd_a_log, ssd_d, ssd_norm, w_out_c, final_norm):
    p = dict(norm_ab=norm_ab, w_in_ab=w_in_ab, gla_w_a2=gla_w_a2, gla_b_a=gla_b_a, gla_norm=gla_norm,
             mlstm_conv_w=mlstm_conv_w, mlstm_conv_b=mlstm_conv_b, mlstm_b_i=mlstm_b_i,
             mlstm_b_f=mlstm_b_f, mlstm_norm=mlstm_norm, w_out_ab=w_out_ab, norm_c=norm_c,
             w_in_c=w_in_c, ssd_conv_w=ssd_conv_w, ssd_conv_b=ssd_conv_b, ssd_dt_bias=ssd_dt_bias,
             ssd_a_log=ssd_a_log, ssd_d=ssd_d, ssd_norm=ssd_norm, w_out_c=w_out_c)
    ab, cc = _prep_params(p)
    fw = final_norm[None]
    n_ab, dec_b = state_gla.shape[0], state_gla.shape[1]
    n_c = state_ssm.shape[0]
    dec_l = x_sample.shape[1]

    xs = jnp.pad(x_sample, ((0, 0), (0, SUBLANES - dec_l), (0, 0)))
    states = dict(
        gla=state_gla, mc=state_mlstm_c,
        mn=state_mlstm_n.reshape(n_ab, dec_b, HEADS, 1, DK),
        mm=state_mlstm_m.reshape(n_ab, dec_b, HEADS, 1, 1),
        mconv=state_mlstm_conv,
        ssm=state_ssm.reshape(n_c, dec_b, SSD_G, SSD_GW, SSD_N),
        sconv=state_ssm_conv)
    prompt_steps = x_prompt.shape[0] * (x_prompt.shape[1] // CHUNK)
    sample_per_step = dec_b // prompt_steps
    cfgs = [dict(c=CHUNK, c_true=CHUNK, valid=CHUNK, transposed=True, nb_ab=1, nb_ssd=1),
            dict(c=SUBLANES, c_true=dec_l, valid=dec_l, transposed=False, nb_ab=sample_per_step,
                 nb_ssd=SAMPLE_SEQS_PER_STEP)]
    ((yp, gla_p, mc_p, mn_p, mm_p, mcv_p, ssm_p, scv_p),
     (ys, gla_s, mc_s, mn_s, mm_s, mcv_s, ssm_s, scv_s)) = _trunk(
        [x_prompt, xs], cfgs, [None, states], ab, cc, fw, tm=1024)
    ys = ys[:, :dec_l]
    return (yp, ys, gla_p, gla_s, mc_p, mc_s, mn_p, mn_s, mm_p, mm_s,
            mcv_p, mcv_s, ssm_p, ssm_s, scv_p, scv_s)
```

```python
import functools
from typing import NamedTuple

import jax
import jax.numpy as jnp
from jax import lax
from jax.experimental import pallas as pl
from jax.experimental.pallas import tpu as pltpu

F32 = jnp.float32
BF16 = jnp.bfloat16

D_MODEL = 2048
D_INNER = 4096
CHUNK = 64
EPS = 1e-6
NEG = -1e30
HEADS = 4
DK = 256
DV = 512
QK = HEADS * DK
WIDTH = HEADS * DV
GLA_LOWRANK = 16
GLA_GATE_TEMP = 16.0
CONV = 4
HIST = CONV - 1
SM_I = 16
SM_F = 20
SSD_HEADS = 64
SSD_P = 64
SSD_G = 8
SSD_HPG = 8
SSD_N = 128
SSD_GW = SSD_HPG * SSD_P
SSD_BC = 2 * SSD_G * SSD_N
SSD_CONV_DIM = D_INNER + SSD_BC
LANES = 128
SUBLANES = 8
SLOT = 64
SLOT_LOG2 = 6

VMEM_LIMIT = 56 * 1024 * 1024
def _cparams(sem):
    return pltpu.CompilerParams(dimension_semantics=sem, vmem_limit_bytes=VMEM_LIMIT)


def _dot(a, b):
    return jnp.dot(a.astype(BF16), b.astype(BF16), preferred_element_type=F32)


def _dot_nt(a, b):
    return lax.dot_general(a.astype(BF16), b.astype(BF16), (((1,), (1,)), ((), ())),
                           preferred_element_type=F32)


def _dot_tn(a, b):
    return lax.dot_general(a.astype(BF16), b.astype(BF16), (((0,), (0,)), ((), ())),
                           preferred_element_type=F32)


def _split3(x):
    x1 = x.astype(BF16).astype(F32)
    r = x - x1
    x2 = r.astype(BF16).astype(F32)
    x3 = (r - x2).astype(BF16).astype(F32)
    return x1, x2, x3


def _dot_m01(m01, x):
    return _dot(jnp.concatenate([m01] * 3, axis=1), jnp.concatenate(_split3(x), axis=0))


def _dot_x01(x, m01):
    return _dot(jnp.concatenate(_split3(x), axis=1), jnp.concatenate([m01] * 3, axis=0))


def _sigmoid(x):
    return 1.0 / (1.0 + jnp.exp(-x))


def _silu(x):
    return x * _sigmoid(x)


def _softplus(x):
    return jnp.maximum(x, 0.0) + jnp.log(1.0 + jnp.exp(-jnp.abs(x)))


def _log_sigmoid(x):
    return -_softplus(-x)


def _rms(x):
    return x * lax.rsqrt(jnp.mean(x * x, axis=-1, keepdims=True) + EPS)


def _tri(c):
    row = lax.broadcasted_iota(jnp.int32, (c, c), 0)
    col = lax.broadcasted_iota(jnp.int32, (c, c), 1)
    return col <= row


def _row_valid(c, valid):
    return lax.broadcasted_iota(jnp.int32, (c, 1), 0) < valid


def _col_valid(c, valid):
    return lax.broadcasted_iota(jnp.int32, (1, c), 1) < valid


NORM_ROWS = 256


def _inproj_kernel(x_ref, nw_ref, w_ref, ws_ref, bs_ref, o_ref, os_ref, xn_ref):
    @pl.when(pl.program_id(1) == 0)
    def _():
        ws = ws_ref[...]
        wh = ws.astype(BF16)
        wl = (ws - wh.astype(F32)).astype(BF16)
        wcat = jnp.concatenate([wh, wl], axis=1)
        nr = min(NORM_ROWS, x_ref.shape[0])
        for r in range(x_ref.shape[0] // nr):
            rows = pl.ds(r * nr, nr)
            xn = _rms(x_ref[rows, :]) * nw_ref[...]
            xh = xn.astype(BF16)
            xl = (xn - xh.astype(F32)).astype(BF16)
            xn_ref[rows, :] = xh
            p = jnp.dot(xh, wcat, preferred_element_type=F32)
            os_ref[rows, :] = (p[:, :LANES] + p[:, LANES:]
                               + jnp.dot(xl, wh, preferred_element_type=F32) + bs_ref[...])

    o_ref[...] = jnp.dot(xn_ref[...], w_ref[...], preferred_element_type=F32)


def _inproj(x, norm_w, w_main, layer, w_small, b_small, *, tm, tn):
    t, d = x.shape
    n = w_main.shape[2]
    return pl.pallas_call(
        _inproj_kernel,
        out_shape=(jax.ShapeDtypeStruct((t, n), F32), jax.ShapeDtypeStruct((t, LANES), F32)),
        grid=(t // tm, n // tn),
        in_specs=[
            pl.BlockSpec((tm, d), lambda i, j: (i, 0)),
            pl.BlockSpec((1, d), lambda i, j: (0, 0)),
            pl.BlockSpec((None, d, tn), lambda i, j: (layer, 0, j)),
            pl.BlockSpec((None, d, LANES), lambda i, j: (layer, 0, 0)),
            pl.BlockSpec((1, LANES), lambda i, j: (0, 0)),
        ],
        out_specs=(pl.BlockSpec((tm, tn), lambda i, j: (i, j)),
                   pl.BlockSpec((tm, LANES), lambda i, j: (i, 0))),
        scratch_shapes=[pltpu.VMEM((tm, d), BF16)],
        compiler_params=_cparams(("parallel", "arbitrary")),
        name="inproj",
    )(x, norm_w, w_main, w_small, b_small)


CAST_COLS = 1024


def _cast_kernel(w_ref, o_ref, *, segs):
    for src, width, dst in segs:
        for off in range(0, width, CAST_COLS):
            n = min(CAST_COLS, width - off)
            o_ref[0, :, dst + off:dst + off + n] = w_ref[0, :, src + off:src + off + n].astype(BF16)


def _cast_weights(w, segs, *, tr):
    nl, rows, cols = w.shape
    n_out = sum(width for _, width, _ in segs)
    return pl.pallas_call(
        functools.partial(_cast_kernel, segs=segs),
        out_shape=jax.ShapeDtypeStruct((nl, rows, n_out), BF16),
        grid=(nl, rows // tr),
        in_specs=[pl.BlockSpec((1, tr, cols), lambda l, i: (l, i, 0))],
        out_specs=pl.BlockSpec((1, tr, n_out), lambda l, i: (l, i, 0)),
        compiler_params=_cparams(("parallel", "parallel")),
        name="cast_weights",
    )(w)


CAST_T_COLS = 1024


def _cast_t_kernel(a_ref, b_ref, o_ref, os_ref, *, shift_from, shift, small):
    j = pl.program_id(1)

    @pl.when(j == 0)
    def _():
        os_ref[0] = jnp.zeros(os_ref.shape[1:], F32)

    lane = lax.broadcasted_iota(jnp.int32, (1, LANES), 1)
    for tile, use_b, lo_, hi_ in small:
        @pl.when(j == tile)
        def _(use_b=use_b, lo_=lo_, hi_=hi_):
            src = b_ref if use_b else a_ref
            os_ref[0] = jnp.where((lane >= lo_) & (lane < hi_),
                                  jnp.transpose(src[0, 0:LANES, :]), os_ref[0])

    def emit(s):
        for k in range(CAST_T_COLS // LANES):
            lo = s + k * LANES
            if lo + LANES <= CAST_T_COLS:
                blk = a_ref[0, lo:lo + LANES, :]
            else:
                blk = jnp.concatenate([a_ref[0, lo:CAST_T_COLS, :],
                                       b_ref[0, 0:lo + LANES - CAST_T_COLS, :]], axis=0)
            o_ref[0, :, k * LANES:(k + 1) * LANES] = jnp.transpose(blk).astype(BF16)

    if shift == 0:
        emit(0)
    else:
        @pl.when(pl.program_id(1) < shift_from)
        def _():
            emit(0)

        @pl.when(pl.program_id(1) >= shift_from)
        def _():
            emit(shift)


def _cast_weights_t(wt, n_out, small, *, shift_from=0, shift=0):
    nl, _, d = wt.shape
    per = CAST_T_COLS // LANES
    return pl.pallas_call(
        functools.partial(_cast_t_kernel, shift_from=shift_from, shift=shift, small=small),
        out_shape=(jax.ShapeDtypeStruct((nl, d, n_out), BF16),
                   jax.ShapeDtypeStruct((nl, d, LANES), F32)),
        grid=(nl, n_out // CAST_T_COLS),
        in_specs=[pl.BlockSpec((1, CAST_T_COLS, d), lambda l, j: (l, j, 0)),
                  pl.BlockSpec((1, LANES, d), lambda l, j: (l, (j + 1) * per, 0))],
        out_specs=(pl.BlockSpec((1, d, CAST_T_COLS), lambda l, j: (l, 0, j)),
                   pl.BlockSpec((1, d, LANES), lambda l, j: (l, 0, 0))),
        compiler_params=_cparams(("parallel", "arbitrary")),
        name="cast_weights_t",
    )(wt, wt)


def _outproj_kernel(a_ref, b_ref, wa_ref, wb_ref, x_ref, o_ref):
    o_ref[...] = (x_ref[...]
                  + jnp.dot(a_ref[...], wa_ref[...], preferred_element_type=F32)
                  + jnp.dot(b_ref[...], wb_ref[...], preferred_element_type=F32))


def _outproj(a, b, a_blk, b_blk, w, layer, x, *, tm, tn):
    t, d = x.shape
    half = D_INNER // 2
    return pl.pallas_call(
        _outproj_kernel,
        out_shape=jax.ShapeDtypeStruct((t, d), F32),
        grid=(t // tm, d // tn),
        in_specs=[
            pl.BlockSpec((tm, half), lambda i, j: (i, a_blk)),
            pl.BlockSpec((tm, half), lambda i, j: (i, b_blk)),
            pl.BlockSpec((None, half, tn), lambda i, j: (layer, 0, j)),
            pl.BlockSpec((None, half, tn), lambda i, j: (layer, 1, j)),
            pl.BlockSpec((tm, tn), lambda i, j: (i, j)),
        ],
        out_specs=pl.BlockSpec((tm, tn), lambda i, j: (i, j)),
        compiler_params=_cparams(("parallel", "arbitrary")),
        name="outproj",
    )(a, b, w, w, x)


def _final_norm_kernel(x_ref, w_ref, o_ref):
    o_ref[...] = _rms(x_ref[...]) * w_ref[...]


def _final_norm(x, w, *, tm):
    t, d = x.shape
    return pl.pallas_call(
        _final_norm_kernel,
        out_shape=jax.ShapeDtypeStruct((t, d), F32),
        grid=(t // tm,),
        in_specs=[pl.BlockSpec((tm, d), lambda i: (i, 0)),
                  pl.BlockSpec((1, d), lambda i: (0, 0))],
        out_specs=pl.BlockSpec((tm, d), lambda i: (i, 0)),
        compiler_params=_cparams(("parallel",)),
        name="final_norm",
    )(x, w)


def _causal_conv(buf_ref, x, w_ref, b_ref, c):
    buf_ref[pl.ds(SUBLANES, c), :] = x
    xp = buf_ref[...]
    out = b_ref[...] + x * w_ref[CONV - 1:CONV, :]
    for s in range(1, CONV):
        out = out + pltpu.roll(xp, s, 0)[SUBLANES:, :] * w_ref[CONV - 1 - s:CONV - s, :]
    buf_ref[pl.ds(SUBLANES - HIST, HIST), :] = buf_ref[pl.ds(SUBLANES + c - HIST, HIST), :]
    return out


def _init_hist(buf_ref, hist):
    buf_ref[pl.ds(SUBLANES - HIST, HIST), :] = hist


def _run_init(first, fn):
    if first is None:
        fn()
    else:
        pl.when(first)(fn)


def _gla_body(q_ref, k_ref, v_ref, z_ref, sm_ref, wa_ref, ba_ref, gn_ref, s0_ref, o_ref, s_ref,
              *, phase, first, c, c_true, valid):
    read_given = first is None and s0_ref is not None
    if phase == "init":
        if not read_given:
            def init():
                if s0_ref is not None:
                    s_ref[0, 0] = s0_ref[0, 0]
                else:
                    s_ref[0, 0] = jnp.zeros((HEADS, DK, DV), F32)
            _run_init(first, init)
        return
    src_ref = s0_ref if read_given else s_ref

    ag = sm_ref[0][:, :GLA_LOWRANK]
    a1 = ag.astype(BF16).astype(F32)
    a2 = (ag - a1).astype(BF16).astype(F32)
    gpre = _dot(jnp.concatenate([a1, a1, a2], axis=1), wa_ref[...]) + ba_ref[...]
    g = _log_sigmoid(gpre) * (1.0 / GLA_GATE_TEMP)
    rv = _row_valid(c, valid) if valid < c else None
    if rv is not None:
        g = jnp.where(rv, g, 0.0)
    tri = _tri(c)
    g3 = jnp.concatenate(_split3(g), axis=0).astype(BF16)
    tri3 = jnp.concatenate([tri.astype(F32)] * 3, axis=1).astype(BF16)
    b = jnp.dot(tri3, g3, preferred_element_type=F32)
    dcol = jnp.exp(lax.dot_general(g3, jnp.ones((3 * c, LANES), BF16), (((0,), (0,)), ((), ())),
                                   preferred_element_type=F32))

    for h in range(HEADS):
        ks = slice(h * DK, (h + 1) * DK)
        vs = slice(h * DV, (h + 1) * DV)
        q = q_ref[0, :, ks] * (DK ** -0.5)
        k = k_ref[0, :, ks]
        if rv is not None:
            k = jnp.where(rv, k, 0.0)
        v = v_ref[0, :, vs]
        bh = b[:, ks]
        mid = bh[c_true // 2:c_true // 2 + 1, :]
        b_last = bh[c - 1:c, :]
        s = src_ref[0, 0, h]
        o = _dot(q * jnp.exp(bh), s)
        att = _dot_nt(q * jnp.exp(bh - mid), k * jnp.exp(mid - bh))
        o = o + _dot(jnp.where(tri, att, 0.0), v)
        dc = dcol[ks, :]
        s_ref[0, 0, h] = (s * jnp.concatenate([dc] * (DV // LANES), axis=1)
                          + _dot_tn(k * jnp.exp(b_last - bh), v))
        o_ref[0, :, vs] = (_rms(o) * gn_ref[:, vs] * _silu(z_ref[0, :, vs])).astype(BF16)


def _mlstm_body(qk_ref, v_ref, og_ref, z_ref, sm_ref, gt_ref, cw_ref, cb_ref, mn_ref, init_refs,
                o_ref, c_ref, n_ref, m_ref, buf, *, phase, first, c, valid):
    has_init = init_refs is not None
    if has_init:
        c0_ref, n0_ref, m0_ref, h0_ref = init_refs
    read_given = first is None and has_init
    if phase == "init":
        def init():
            if has_init:
                if not read_given:
                    c_ref[0, 0] = c0_ref[0, 0]
                    n_ref[0, 0] = n0_ref[0, 0]
                    m_ref[0, 0] = jnp.broadcast_to(m0_ref[0, 0], (HEADS, 1, LANES))
                _init_hist(buf, h0_ref[0, 0])
            else:
                c_ref[0, 0] = jnp.zeros((HEADS, DK, DV), F32)
                n_ref[0, 0] = jnp.zeros((HEADS, 1, DK), F32)
                m_ref[0, 0] = jnp.zeros((HEADS, 1, LANES), F32)
                _init_hist(buf, jnp.zeros((HIST, 2 * QK), F32))
        _run_init(first, init)
        return
    c_src, n_src, m_src = (c0_ref, n0_ref, m0_ref) if read_given else (c_ref, n_ref, m_ref)

    qk = _silu(_causal_conv(buf, qk_ref[0], cw_ref, cb_ref, c))
    sm = sm_ref[0]
    gt = gt_ref[0, 0]
    rv = _row_valid(c, valid) if valid < c else None
    cv = _col_valid(c, valid) if valid < c else None
    tri = _tri(c)
    row = lax.broadcasted_iota(jnp.int32, (c, c), 0)
    col = lax.broadcasted_iota(jnp.int32, (c, c), 1)
    triu = row <= col

    for h in range(HEADS):
        vs = slice(h * DV, (h + 1) * DV)
        q = qk[:, h * DK:(h + 1) * DK]
        k = qk[:, QK + h * DK:QK + (h + 1) * DK] * (DK ** -0.5)
        v = v_ref[0, :, vs]
        ig_c = sm[:, SM_I + h:SM_I + h + 1]
        lf_c = _log_sigmoid(sm[:, SM_F + h:SM_F + h + 1])
        ig_r = gt[h:h + 1, :]
        lf_r = _log_sigmoid(gt[HEADS + h:HEADS + h + 1, :])
        if rv is not None:
            lf_c = jnp.where(rv, lf_c, 0.0)
            lf_r = jnp.where(cv, lf_r, 0.0)
            ig_c = jnp.where(rv, ig_c, NEG)
            ig_r = jnp.where(cv, ig_r, NEG)
        fcum_c = jnp.sum(jnp.where(tri, lf_r, 0.0), axis=1, keepdims=True)
        fcum_r = jnp.sum(jnp.where(triu, lf_c, 0.0), axis=0, keepdims=True)
        m_prev = m_src[0, 0, h][:, 0:1]
        dlog = jnp.where(tri, fcum_c - fcum_r + ig_r, NEG)
        inter = fcum_c + m_prev
        m_i = jnp.maximum(inter, jnp.max(dlog, axis=1, keepdims=True))
        w_inter = jnp.exp(inter - m_i)
        qkm = _dot_nt(q, k) * jnp.exp(dlog - m_i)
        cm = c_src[0, 0, h]
        nm = n_src[0, 0, h]
        num = _dot(qkm, v) + w_inter * _dot(q, cm)
        den = (jnp.sum(qkm, axis=1, keepdims=True)
               + w_inter * jnp.sum(q * nm, axis=1, keepdims=True))
        den = jnp.maximum(jnp.abs(den), jnp.exp(-m_i))
        hh = num / den
        m_new = m_i[c - 1:c, :]
        f_last = fcum_c[c - 1:c, :]
        w_j = jnp.exp(f_last - fcum_c + ig_c - m_new)
        decay = jnp.exp(f_last + m_prev - m_new)
        kw = w_j * k
        c_ref[0, 0, h] = decay * cm + _dot_tn(kw, v)
        n_ref[0, 0, h] = decay * nm + jnp.sum(kw, axis=0, keepdims=True)
        m_ref[0, 0, h] = jnp.broadcast_to(m_new, (1, LANES))
        hm = _sigmoid(og_ref[0, :, vs]) * hh
        o_ref[0, :, WIDTH + h * DV:WIDTH + (h + 1) * DV] = (
            _rms(hm) * mn_ref[:, vs] * _silu(z_ref[0, :, vs])).astype(BF16)


N_AB_W = 6
N_AB_ROW = 10
N_AB_INIT = 5
N_AB_STATE_OUT = 4


class _Group(NamedTuple):
    nb: int
    c: int
    c_true: int
    valid: int
    chunked: bool
    has_init: bool
    has_prev: bool


def _ab_kernel(*refs, groups):
    refs = list(refs)
    wa_ref, ba_ref, gn_ref, cw_ref, cb_ref, mn_ref = refs[:N_AB_W]
    refs = refs[N_AB_W:]
    ins = []
    for g in groups:
        rows, refs = refs[:N_AB_ROW], refs[N_AB_ROW:]
        init = []
        if g.has_init:
            init, refs = refs[:N_AB_INIT], refs[N_AB_INIT:]
        if g.has_prev:
            refs = refs[N_AB_STATE_OUT:]
        ins.append((rows, init))
    outs = []
    for g in groups:
        outs.append(refs[:1 + N_AB_STATE_OUT])
        refs = refs[1 + N_AB_STATE_OUT:]
    for phase in ("init", "compute"):
        for g, (rows, init), out, buf in zip(groups, ins, outs, refs):
            q_ref, k_ref, v_ref, z_ref, sm_ref, qkm_ref, vm_ref, og_ref, zm_ref, gt_ref = rows
            o_ref, s_ref, c_ref, n_ref, m_ref = out
            first = pl.program_id(1) == 0 if g.chunked else None
            for bi in range(g.nb):
                seq = slice(bi, bi + 1)
                row = lambda r: r.at[seq]
                st = lambda r: r.at[:, seq]
                s0_ref = st(init[0]) if g.has_init else None
                init_refs = [st(r) for r in init[1:]] if g.has_init else None
                _gla_body(row(q_ref), row(k_ref), row(v_ref), row(z_ref), row(sm_ref), wa_ref,
                          ba_ref, gn_ref, s0_ref, row(o_ref), st(s_ref), phase=phase, first=first,
                          c=g.c, c_true=g.c_true, valid=g.valid)
                _mlstm_body(row(qkm_ref), row(vm_ref), row(og_ref), row(zm_ref), row(sm_ref),
                            row(gt_ref), cw_ref, cb_ref, mn_ref, init_refs, row(o_ref), st(c_ref),
                            st(n_ref), st(m_ref), buf.at[bi], phase=phase, first=first, c=g.c,
                            valid=g.valid)


def _ab_scan(data, P, *, layer, n_layers):
    d0 = data[0]
    grid = (d0['main'].shape[0] // d0['nb'], d0['main'].shape[1] // d0['c'])
    base = (2 * QK + 2 * WIDTH) // WIDTH
    const = lambda b, n: (0, 0)
    in_specs = [
        pl.BlockSpec((3 * GLA_LOWRANK, QK), const),
        pl.BlockSpec((1, QK), const),
        pl.BlockSpec((1, WIDTH), const),
        pl.BlockSpec((CONV, 2 * QK), const),
        pl.BlockSpec((1, 2 * QK), const),
        pl.BlockSpec((1, WIDTH), const),
    ]
    args = [P['wa3'], P['ba'], P['gnorm'], P['cw'], P['cb'], P['mnorm']]
    groups, out_shape, out_specs, scratch, aliases = [], [], [], [], {}
    n_out = 0
    for gi, d in enumerate(data):
        main, nb, c = d['main'], d['nb'], d['c']
        bsz, L, _ = main.shape
        chunked = gi == 0
        if chunked:
            seq = lambda b, n: b
            chunk = lambda b, n: n
        else:
            assert L == c and bsz // nb == grid[0] * grid[1]
            seq = lambda b, n: b * grid[1] + n
            chunk = lambda b, n: 0
        row = lambda blk, seq=seq, chunk=chunk: (lambda b, n: (seq(b, n), chunk(b, n), blk))
        state = lambda b, n, seq=seq: (layer, seq(b, n), 0, 0, 0)
        in_specs += [
            pl.BlockSpec((nb, c, QK), row(0)),
            pl.BlockSpec((nb, c, QK), row(1)),
            pl.BlockSpec((nb, c, WIDTH), row(2 * QK // WIDTH)),
            pl.BlockSpec((nb, c, WIDTH), row(2 * QK // WIDTH + 1)),
            pl.BlockSpec((nb, c, LANES), row(0)),
            pl.BlockSpec((nb, c, 2 * QK), row(base)),
            pl.BlockSpec((nb, c, WIDTH), row(base + 1)),
            pl.BlockSpec((nb, c, WIDTH), row(base + 2)),
            pl.BlockSpec((nb, c, WIDTH), row(base + 3)),
            pl.BlockSpec((nb, 1, 2 * HEADS, c),
                         lambda b, n, seq=seq, chunk=chunk: (seq(b, n), chunk(b, n), 0, 0)),
        ]
        args += [main, main, main, main, d['small'], main, main, main, main, d['gates_t']]
        has_init = d['init'] is not None
        has_prev = d['prev'] is not None
        if has_init:
            in_specs += [
                pl.BlockSpec((1, nb, HEADS, DK, DV), state),
                pl.BlockSpec((1, nb, HEADS, DK, DV), state),
                pl.BlockSpec((1, nb, HEADS, 1, DK), state),
                pl.BlockSpec((1, nb, HEADS, 1, 1), state),
                pl.BlockSpec((1, nb, HIST, 2 * QK), lambda b, n, seq=seq: (layer, seq(b, n), 0, 0)),
            ]
            args += list(d['init'])
        if has_prev:
            for i, p in enumerate(d['prev']):
                aliases[len(args)] = n_out + 1 + i
                in_specs.append(pl.BlockSpec(memory_space=pl.ANY))
                args.append(p)
        out_shape += [jax.ShapeDtypeStruct((bsz, L, 2 * WIDTH), BF16),
                      jax.ShapeDtypeStruct((n_layers, bsz, HEADS, DK, DV), F32),
                      jax.ShapeDtypeStruct((n_layers, bsz, HEADS, DK, DV), F32),
                      jax.ShapeDtypeStruct((n_layers, bsz, HEADS, 1, DK), F32),
                      jax.ShapeDtypeStruct((n_layers, bsz, HEADS, 1, LANES), F32)]
        out_specs += [pl.BlockSpec((nb, c, 2 * WIDTH), row(0)),
                      pl.BlockSpec((1, nb, HEADS, DK, DV), state),
                      pl.BlockSpec((1, nb, HEADS, DK, DV), state),
                      pl.BlockSpec((1, nb, HEADS, 1, DK), state),
                      pl.BlockSpec((1, nb, HEADS, 1, LANES), state)]
        n_out += 1 + N_AB_STATE_OUT
        scratch.append(pltpu.VMEM((nb, SUBLANES + c, 2 * QK), F32))
        groups.append(_Group(nb=nb, c=c, c_true=d['c_true'], valid=d['valid'], chunked=chunked,
                             has_init=has_init, has_prev=has_prev))
    res = pl.pallas_call(
        functools.partial(_ab_kernel, groups=tuple(groups)),
        out_shape=tuple(out_shape),
        grid=grid,
        in_specs=in_specs,
        out_specs=tuple(out_specs),
        scratch_shapes=scratch,
        input_output_aliases=aliases,
        compiler_params=_cparams(("parallel", "arbitrary")),
        name="ab_scan",
    )(*args)
    per = 1 + N_AB_STATE_OUT
    return [res[i * per:(i + 1) * per] for i in range(len(data))]


def _pad_rows(a, rows):
    if a.shape[0] == rows:
        return a
    return jnp.concatenate([a, jnp.zeros((rows - a.shape[0], a.shape[1]), a.dtype)], axis=0)


N_SSD_W = 7
N_SSD_ROW = 4
N_SSD_INIT = 3


class _SsdGroup(NamedTuple):
    nb: int
    c: int
    valid: int
    chunked: bool
    has_init: bool
    has_prev: bool
    transposed: bool
    last: int


def _ssd_kernel(*refs, groups):
    refs = list(refs)
    weights, refs = refs[:N_SSD_W], refs[N_SSD_W:]
    ins = []
    for g in groups:
        rows, refs = refs[:N_SSD_ROW], refs[N_SSD_ROW:]
        init = []
        if g.has_init:
            init, refs = refs[:N_SSD_INIT], refs[N_SSD_INIT:]
        if g.has_prev:
            refs = refs[1:]
        ins.append((rows, init))
    outs = []
    for g in groups:
        outs.append(refs[:2])
        refs = refs[2:]
    scratch = []
    for g in groups:
        n = 3 if g.transposed else 2
        scratch.append(refs[:n])
        refs = refs[n:]
    for phase in ("init", "compute", "final"):
        for g, (rows, init), (o_ref, s_ref), scr in zip(groups, ins, outs, scratch):
            first = pl.program_id(1) == 0 if g.chunked else None
            for bi in range(g.nb):
                seq = slice(bi, bi + 1)
                _ssd_body(*[r.at[seq] for r in rows], *weights, [r.at[:, seq] for r in init],
                          o_ref.at[seq], s_ref.at[:, seq], *[r.at[bi] for r in scr], phase=phase,
                          first=first, c=g.c, valid=g.valid, transposed=g.transposed, last=g.last)


def _ssd_body(z_ref, xp_ref, bcp_ref, sm_ref, cwx_ref, cwbc_ref, cbx_ref, cbbc_ref,
              alog_ref, d_ref, nw_ref, init_refs, o_ref, s_ref, xbuf, bcbuf, st_ref=None,
              *, phase, first, c, valid, transposed, last):
    has_init = bool(init_refs)
    if has_init:
        s0_ref, hx_ref, hbc_ref = init_refs
    read_given = first is None and has_init and not transposed
    if phase == "init":
        def init():
            if has_init:
                _init_hist(xbuf, hx_ref[0, 0])
                _init_hist(bcbuf, hbc_ref[0, 0])
                if transposed:
                    for g in range(SSD_G):
                        st_ref[g] = jnp.transpose(s0_ref[0, 0, g])
                elif not read_given:
                    s_ref[0, 0] = s0_ref[0, 0]
            else:
                _init_hist(xbuf, jnp.zeros((HIST, D_INNER), F32))
                _init_hist(bcbuf, jnp.zeros((HIST, SSD_BC), F32))
                if transposed:
                    st_ref[...] = jnp.zeros((SSD_G, SSD_N, SSD_GW), F32)
                else:
                    s_ref[0, 0] = jnp.zeros((SSD_G, SSD_GW, SSD_N), F32)
        _run_init(first, init)
        return
    if phase == "final":
        if transposed:
            @pl.when(pl.program_id(1) == last)
            def _():
                for g in range(SSD_G):
                    s_ref[0, 0, g] = jnp.transpose(st_ref[g])
        return
    src_ref = s0_ref if read_given else s_ref

    xall = _silu(_causal_conv(xbuf, xp_ref[0], cwx_ref, cbx_ref, c))
    bcall = _silu(_causal_conv(bcbuf, bcp_ref[0], cwbc_ref, cbbc_ref, c))
    dt_all = _softplus(sm_ref[0][:, :SSD_HEADS])
    if valid < c:
        dt_all = jnp.where(_row_valid(c, valid), dt_all, 0.0)
    a_all = -jnp.exp(alog_ref[...])
    tri = _tri(c).astype(F32)
    acum_all = _dot_m01(tri, dt_all * a_all)

    slot = min(c, SLOT)
    slot_log2 = slot.bit_length() - 1
    hw = SSD_HPG * slot
    per_mm = min(SSD_HPG, max(1, LANES // slot))
    iota = lambda shape, axis: lax.broadcasted_iota(jnp.int32, shape, axis)
    e_p = (lax.shift_right_logical(iota((SSD_HPG, SSD_GW), 1), SLOT_LOG2)
           == iota((SSD_HPG, SSD_GW), 0)).astype(F32)
    if slot == SSD_P:
        e = e_p
    else:
        e_j = (lax.shift_right_logical(iota((SSD_HPG, hw), 1), slot_log2)
               == iota((SSD_HPG, hw), 0)).astype(F32)
        e = jnp.concatenate([e_p, e_j], axis=1)
    jpos = jnp.bitwise_and(iota((c, hw), 1), slot - 1)
    causal = jpos <= iota((c, hw), 0)
    diag = jpos == iota((c, hw), 0)
    sel = (lax.shift_right_logical(iota((per_mm * slot, per_mm * SSD_P), 0), slot_log2)
           == lax.shift_right_logical(iota((per_mm * slot, per_mm * SSD_P), 1), SLOT_LOG2))
    tile_dtype = BF16 if slot % (2 * SUBLANES) == 0 else F32

    for g in range(SSD_G):
        gs = slice(g * SSD_GW, (g + 1) * SSD_GW)
        hs = slice(g * SSD_HPG, (g + 1) * SSD_HPG)
        x = xall[:, gs]
        bm = bcall[:, g * SSD_N:(g + 1) * SSD_N]
        cm = bcall[:, SSD_G * SSD_N + g * SSD_N:SSD_G * SSD_N + (g + 1) * SSD_N]
        ex = _dot_x01(jnp.concatenate([acum_all[:, hs], dt_all[:, hs]], axis=0), e)
        a_exp = ex[:c, :SSD_GW]
        dt_exp = ex[c:, :SSD_GW]
        a_j = a_exp if slot == SSD_P else ex[:c, SSD_GW:]
        a_row = jnp.sum(jnp.where(diag, a_j, 0.0), axis=0, keepdims=True)
        dec = jnp.exp(jnp.where(causal, a_j - a_row, NEG))
        bm_t = jnp.concatenate([_pad_rows(bm, slot)] * SSD_HPG, axis=0)
        m = (_dot_nt(cm, bm_t) * dec).astype(BF16)
        xdt = _pad_rows(x * dt_exp, slot).astype(tile_dtype)
        ys = []
        for pr in range(SSD_HPG // per_mm):
            xp = xdt[:, pr * per_mm * SSD_P:(pr + 1) * per_mm * SSD_P]
            xblk = jnp.where(sel, jnp.concatenate([xp] * per_mm, axis=0), 0.0).astype(BF16)
            ys.append(jnp.dot(m[:, pr * per_mm * slot:(pr + 1) * per_mm * slot], xblk,
                              preferred_element_type=F32))
        y = ys[0] if len(ys) == 1 else jnp.concatenate(ys, axis=1)
        a_last = a_exp[c - 1:c, :]
        wx = x * (jnp.exp(a_last - a_exp) * dt_exp)
        if transposed:
            st = st_ref[g]
            y = y + _dot(cm, st) * jnp.exp(a_exp)
            st_ref[g] = st * jnp.exp(a_last) + _dot_tn(bm, wx)
        else:
            s = src_ref[0, 0, g]
            y = y + _dot_nt(cm, s) * jnp.exp(a_exp)
            upd = _dot_tn(wx, bm)
            sdec = jnp.exp(acum_all[c - 1:c, hs])
            for hh in range(SSD_HPG):
                rs = slice(hh * SSD_P, (hh + 1) * SSD_P)
                s_ref[0, 0, g, rs, :] = s[rs, :] * sdec[:, hh:hh + 1] + upd[rs, :]
        y = y + d_ref[:, gs] * x
        y = y * _silu(z_ref[0, :, gs])
        o_ref[0, :, gs] = (_rms(y) * nw_ref[:, gs]).astype(BF16)


def _ssd_scan(data, P, *, layer, n_layers):
    d0 = data[0]
    grid = (d0['main'].shape[0] // d0['nb'], d0['main'].shape[1] // d0['c'])
    bc_blk = 2 * D_INNER // SSD_BC
    bc_w = D_INNER // SSD_BC
    const = lambda blk: (lambda b, n: (0, blk))
    in_specs = [
        pl.BlockSpec((CONV, D_INNER), const(0)),
        pl.BlockSpec((CONV, SSD_BC), const(bc_w)),
        pl.BlockSpec((1, D_INNER), const(0)),
        pl.BlockSpec((1, SSD_BC), const(bc_w)),
        pl.BlockSpec((1, SSD_HEADS), const(0)),
        pl.BlockSpec((1, D_INNER), const(0)),
        pl.BlockSpec((1, D_INNER), const(0)),
    ]
    args = [P['cw'], P['cw'], P['cb'], P['cb'], P['alog'], P['d_exp'], P['normw']]
    groups, out_shape, out_specs, scratch, aliases = [], [], [], [], {}
    for gi, d in enumerate(data):
        main, nb, c = d['main'], d['nb'], d['c']
        bsz, L, _ = main.shape
        chunked = gi == 0
        if chunked:
            seq = lambda b, n: b
            chunk = lambda b, n: n
        else:
            assert L == c and bsz // nb == grid[0] * grid[1]
            seq = lambda b, n: b * grid[1] + n
            chunk = lambda b, n: 0
        row = lambda blk, seq=seq, chunk=chunk: (lambda b, n: (seq(b, n), chunk(b, n), blk))
        state = lambda b, n, seq=seq: (layer, seq(b, n), 0, 0, 0)
        hist = lambda blk, seq=seq: (lambda b, n: (layer, seq(b, n), 0, blk))
        in_specs += [
            pl.BlockSpec((nb, c, D_INNER), row(0)),
            pl.BlockSpec((nb, c, D_INNER), row(1)),
            pl.BlockSpec((nb, c, SSD_BC), row(bc_blk)),
            pl.BlockSpec((nb, c, LANES), row(0)),
        ]
        args += [main, main, main, d['small']]
        has_init = d['init'] is not None
        has_prev = d['prev'] is not None
        if has_init:
            s0, conv0 = d['init']
            in_specs += [
                pl.BlockSpec((1, nb, SSD_G, SSD_GW, SSD_N), state),
                pl.BlockSpec((1, nb, HIST, D_INNER), hist(0)),
                pl.BlockSpec((1, nb, HIST, SSD_BC), hist(bc_w)),
            ]
            args += [s0, conv0, conv0]
        if has_prev:
            aliases[len(args)] = 2 * gi + 1
            in_specs.append(pl.BlockSpec(memory_space=pl.ANY))
            args.append(d['prev'])
        out_shape += [jax.ShapeDtypeStruct((bsz, L, D_INNER), BF16),
                      jax.ShapeDtypeStruct((n_layers, bsz, SSD_G, SSD_GW, SSD_N), F32)]
        out_specs += [pl.BlockSpec((nb, c, D_INNER), row(0)),
                      pl.BlockSpec((1, nb, SSD_G, SSD_GW, SSD_N), state)]
        scratch += [pltpu.VMEM((nb, SUBLANES + c, D_INNER), F32),
                    pltpu.VMEM((nb, SUBLANES + c, SSD_BC), F32)]
        if d['transposed']:
            scratch.append(pltpu.VMEM((nb, SSD_G, SSD_N, SSD_GW), F32))
        groups.append(_SsdGroup(nb=nb, c=c, valid=d['valid'], chunked=chunked, has_init=has_init,
                                has_prev=has_prev, transposed=d['transposed'], last=L // c - 1))
    res = pl.pallas_call(
        functools.partial(_ssd_kernel, groups=tuple(groups)),
        out_shape=tuple(out_shape),
        grid=grid,
        in_specs=in_specs,
        out_specs=tuple(out_specs),
        scratch_shapes=scratch,
        input_output_aliases=aliases,
        compiler_params=_cparams(("parallel", "arbitrary")),
        name="ssd_scan",
    )(*args)
    return [res[2 * i:2 * i + 2] for i in range(len(data))]


def _prep_params(p):
    n_ab = p['w_in_ab'].shape[0]
    n_c = p['w_in_c'].shape[0]
    o_ag = 2 * QK + 2 * WIDTH
    o_m = o_ag + GLA_LOWRANK
    o_if = o_m + 2 * QK + 3 * WIDTH
    ab, cc = [], []
    wt_ab = jnp.swapaxes(p['w_in_ab'], 1, 2)
    wt_c = jnp.swapaxes(p['w_in_c'], 1, 2)
    n_main_ab = o_if - GLA_LOWRANK
    o_dt = D_INNER + SSD_CONV_DIM
    w_main_ab, w_small_ab = _cast_weights_t(
        wt_ab, n_main_ab,
        ((o_ag // CAST_T_COLS, False, 0, GLA_LOWRANK),
         (n_main_ab // CAST_T_COLS - 1, True, SM_I, SM_I + 2 * HEADS)),
        shift_from=o_ag // CAST_T_COLS, shift=GLA_LOWRANK)
    w_main_c, w_small_c = _cast_weights_t(
        wt_c, o_dt, ((o_dt // CAST_T_COLS - 1, True, 0, SSD_HEADS),))
    w_out_ab = _cast_weights(p['w_out_ab'], ((0, D_MODEL, 0),), tr=512)
    w_out_c = _cast_weights(p['w_out_c'], ((0, D_MODEL, 0),), tr=512)
    for i in range(n_ab):
        b_small = jnp.zeros((1, LANES), F32)
        b_small = b_small.at[0, SM_I:SM_I + HEADS].set(p['mlstm_b_i'][i])
        b_small = b_small.at[0, SM_F:SM_F + HEADS].set(p['mlstm_b_f'][i])
        wa = p['gla_w_a2'][i]
        wa_hi = wa.astype(BF16)
        wa_lo = (wa - wa_hi.astype(F32)).astype(BF16)
        ab.append(dict(
            norm=p['norm_ab'][i][None], w_main=w_main_ab, w_small=w_small_ab, b_small=b_small,
            wa3=jnp.concatenate([wa_hi, wa_lo, wa_hi], axis=0),
            ba=p['gla_b_a'][i][None], gnorm=p['gla_norm'][i][None],
            cw=p['mlstm_conv_w'][i], cb=p['mlstm_conv_b'][i][None],
            mnorm=p['mlstm_norm'][i][None], w_out=w_out_ab))
    for i in range(n_c):
        b_small = jnp.zeros((1, LANES), F32).at[0, :SSD_HEADS].set(p['ssd_dt_bias'][i])
        cc.append(dict(
            norm=p['norm_c'][i][None], w_main=w_main_c, w_small=w_small_c,
            b_small=b_small, cw=p['ssd_conv_w'][i], cb=p['ssd_conv_b'][i][None],
            alog=p['ssd_a_log'][i][None],
            d_exp=jnp.repeat(p['ssd_d'][i], SSD_P)[None],
            normw=p['ssd_norm'][i][None], w_out=w_out_c))
    return ab, cc


def _trunk(xs, cfgs, states, ab, cc, final_w, *, tm):
    n_ab, n_c = len(ab), len(cc)
    ng = len(xs)
    dims = [x.shape[:2] for x in xs]
    xt = [x.reshape(-1, D_MODEL) for x in xs]
    ab_o = [None] * ng
    ssm_o = [None] * ng
    mconv = [[] for _ in xs]
    sconv = [[] for _ in xs]
    row0 = [L - cfg['c'] + cfg['valid'] - HIST for (_, L), cfg in zip(dims, cfgs)]
    for layer in range(n_ab + n_c):
        i = layer // 2
        P = ab[i] if layer % 2 == 0 else cc[i]
        mains, smalls = [], []
        for g in range(ng):
            main, small = _inproj(xt[g], P['norm'], P['w_main'], i, P['w_small'], P['b_small'],
                                  tm=min(tm, xt[g].shape[0]), tn=1024)
            mains.append(main.reshape(dims[g] + (-1,)))
            smalls.append(small.reshape(dims[g] + (LANES,)))
        if layer % 2 == 0:
            data = []
            for g in range(ng):
                (bsz, L), cfg, st = dims[g], cfgs[g], states[g]
                c = cfg['c']
                gates_t = jnp.swapaxes(smalls[g][:, :, SM_I:SM_I + 2 * HEADS].reshape(
                    bsz, L // c, c, 2 * HEADS), 2, 3)
                init = None if st is None else (st['gla'], st['mc'], st['mn'], st['mm'], st['mconv'])
                prev = None if ab_o[g] is None else ab_o[g][1:]
                data.append(dict(main=mains[g], small=smalls[g], gates_t=gates_t, init=init,
                                 prev=prev, nb=cfg['nb'], c=c, c_true=cfg['c_true'],
                                 valid=cfg['valid']))
                mconv[g].append(mains[g][:, row0[g]:row0[g] + HIST,
                                         2 * QK + 2 * WIDTH:2 * QK + 2 * WIDTH + 2 * QK])
            ab_o = _ab_scan(data, P, layer=i, n_layers=n_ab)
            ys = [o[0] for o in ab_o]
        else:
            data = []
            for g in range(ng):
                cfg, st = cfgs[g], states[g]
                init = None if st is None else (st['ssm'], st['sconv'])
                data.append(dict(main=mains[g], small=smalls[g], init=init, prev=ssm_o[g],
                                 nb=cfg['nb'], c=cfg['c'], valid=cfg['valid'],
                                 transposed=cfg['transposed']))
                sconv[g].append(mains[g][:, row0[g]:row0[g] + HIST, D_INNER:D_INNER + SSD_CONV_DIM])
            res = _ssd_scan(data, P, layer=i, n_layers=n_c)
            ys = [r[0] for r in res]
            ssm_o = [r[1] for r in res]
        for g in range(ng):
            y2 = ys[g].reshape(-1, D_INNER)
            xt[g] = _outproj(y2, y2, 0, 1, P['w_out'], i, xt[g], tm=min(tm, xt[g].shape[0]),
                             tn=512)
    outs = []
    for g in range(ng):
        bsz, L = dims[g]
        y = _final_norm(xt[g], final_w, tm=min(tm, 512, xt[g].shape[0])).reshape(bsz, L, D_MODEL)
        _, gla_o, mc_o, mn_o, mm_o = ab_o[g]
        outs.append((y, gla_o, mc_o, mn_o[:, :, :, 0, :], mm_o[:, :, :, 0, 0], jnp.stack(mconv[g]),
                     ssm_o[g].reshape(n_c, bsz, SSD_HEADS, SSD_P, SSD_N), jnp.stack(sconv[g])))
    return outs


def kernel(x_prompt, x_sample, state_gla, state_mlstm_c, state_mlstm_n, state_mlstm_m, state_mlstm_conv, state_ssm, state_ssm_conv, norm_ab, w_in_ab, gla_w_a2, gla_b_a, gla_norm, mlstm_conv_w, mlstm_conv_b, mlstm_b_i, mlstm_b_f, mlstm_norm, w_out_ab, norm_c, w_in_c, ssd_conv_w, ssd_conv_b, ssd_dt_bias, ssd_a_log, ssd_d, ssd_norm, w_out_c, final_norm):
    p = dict(norm_ab=norm_ab, w_in_ab=w_in_ab, gla_w_a2=gla_w_a2, gla_b_a=gla_b_a, gla_norm=gla_norm,
             mlstm_conv_w=mlstm_conv_w, mlstm_conv_b=mlstm_conv_b, mlstm_b_i=mlstm_b_i,
             mlstm_b_f=mlstm_b_f, mlstm_norm=mlstm_norm, w_out_ab=w_out_ab, norm_c=norm_c,
             w_in_c=w_in_c, ssd_conv_w=ssd_conv_w, ssd_conv_b=ssd_conv_b, ssd_dt_bias=ssd_dt_bias,
             ssd_a_log=ssd_a_log, ssd_d=ssd_d, ssd_norm=ssd_norm, w_out_c=w_out_c)
    ab, cc = _prep_params(p)
    fw = final_norm[None]
    n_ab, dec_b = state_gla.shape[0], state_gla.shape[1]
    n_c = state_ssm.shape[0]
    dec_l = x_sample.shape[1]

    xs = jnp.pad(x_sample, ((0, 0), (0, SUBLANES - dec_l), (0, 0)))
    states = dict(
        gla=state_gla, mc=state_mlstm_c,
        mn=state_mlstm_n.reshape(n_ab, dec_b, HEADS, 1, DK),
        mm=state_mlstm_m.reshape(n_ab, dec_b, HEADS, 1, 1),
        mconv=state_mlstm_conv,
        ssm=state_ssm.reshape(n_c, dec_b, SSD_G, SSD_GW, SSD_N),
        sconv=state_ssm_conv)
    prompt_steps = x_prompt.shape[0] * (x_prompt.shape[1] // CHUNK)
    sample_per_step = dec_b // prompt_steps
    cfgs = [dict(c=CHUNK, c_true=CHUNK, valid=CHUNK, transposed=True, nb=1),
            dict(c=SUBLANES, c_true=dec_l, valid=dec_l, transposed=False, nb=sample_per_step)]
    ((yp, gla_p, mc_p, mn_p, mm_p, mcv_p, ssm_p, scv_p),
     (ys, gla_s, mc_s, mn_s, mm_s, mcv_s, ssm_s, scv_s)) = _trunk(
        [x_prompt, xs], cfgs, [None, states], ab, cc, fw, tm=1024)
    ys = ys[:, :dec_l]
    return (yp, ys, gla_p, gla_s, mc_p, mc_s, mn_p, mn_s, mm_p, mm_s,
            mcv_p, mcv_s, ssm_p, ssm_s, scv_p, scv_s)
```

```python
import functools
from typing import NamedTuple

import jax
import jax.numpy as jnp
from jax import lax
from jax.experimental import pallas as pl
from jax.experimental.pallas import tpu as pltpu

F32 = jnp.float32
BF16 = jnp.bfloat16

D_MODEL = 2048
D_INNER = 4096
CHUNK = 64
EPS = 1e-6
NEG = -1e30
HEADS = 4
DK = 256
DV = 512
QK = HEADS * DK
WIDTH = HEADS * DV
GLA_LOWRANK = 16
GLA_GATE_TEMP = 16.0
CONV = 4
HIST = CONV - 1
SM_I = 16
SM_F = 20
SSD_HEADS = 64
SSD_P = 64
SSD_G = 8
SSD_HPG = 8
SSD_N = 128
SSD_GW = SSD_HPG * SSD_P
SSD_BC = 2 * SSD_G * SSD_N
SSD_CONV_DIM = D_INNER + SSD_BC
LANES = 128
SUBLANES = 8
SLOT = 64
SLOT_LOG2 = 6

VMEM_LIMIT = 56 * 1024 * 1024
PROJ_TM = 1024
INPROJ_TN = 1024
OUTPROJ_TN = 512
NORM_TM = 512
def _cparams(sem):
    return pltpu.CompilerParams(dimension_semantics=sem, vmem_limit_bytes=VMEM_LIMIT)


def _dot(a, b):
    return jnp.dot(a.astype(BF16), b.astype(BF16), preferred_element_type=F32)


def _dot_nt(a, b):
    return lax.dot_general(a.astype(BF16), b.astype(BF16), (((1,), (1,)), ((), ())),
                           preferred_element_type=F32)


def _dot_tn(a, b):
    return lax.dot_general(a.astype(BF16), b.astype(BF16), (((0,), (0,)), ((), ())),
                           preferred_element_type=F32)


def _split3(x):
    x1 = x.astype(BF16).astype(F32)
    r = x - x1
    x2 = r.astype(BF16).astype(F32)
    x3 = (r - x2).astype(BF16).astype(F32)
    return x1, x2, x3


def _dot_m01(m01, x):
    return _dot(jnp.concatenate([m01] * 3, axis=1), jnp.concatenate(_split3(x), axis=0))


def _dot_x01(x, m01):
    return _dot(jnp.concatenate(_split3(x), axis=1), jnp.concatenate([m01] * 3, axis=0))


def _sigmoid(x):
    return 1.0 / (1.0 + jnp.exp(-x))


def _silu(x):
    return x * _sigmoid(x)


def _softplus(x):
    return jnp.maximum(x, 0.0) + jnp.log(1.0 + jnp.exp(-jnp.abs(x)))


def _log_sigmoid(x):
    return -_softplus(-x)


def _rms(x):
    return x * lax.rsqrt(jnp.mean(x * x, axis=-1, keepdims=True) + EPS)


def _tri(c):
    row = lax.broadcasted_iota(jnp.int32, (c, c), 0)
    col = lax.broadcasted_iota(jnp.int32, (c, c), 1)
    return col <= row


def _row_valid(c, valid):
    return lax.broadcasted_iota(jnp.int32, (c, 1), 0) < valid


def _col_valid(c, valid):
    return lax.broadcasted_iota(jnp.int32, (1, c), 1) < valid


NORM_ROWS = 256


def _inproj_kernel(x_ref, nw_ref, w_ref, ws_ref, bs_ref, o_ref, os_ref, xn_ref):
    @pl.when(pl.program_id(1) == 0)
    def _():
        ws = ws_ref[...]
        wh = ws.astype(BF16)
        wl = (ws - wh.astype(F32)).astype(BF16)
        wcat = jnp.concatenate([wh, wl], axis=1)
        nr = min(NORM_ROWS, x_ref.shape[0])
        for r in range(x_ref.shape[0] // nr):
            rows = pl.ds(r * nr, nr)
            xn = _rms(x_ref[rows, :]) * nw_ref[...]
            xh = xn.astype(BF16)
            xl = (xn - xh.astype(F32)).astype(BF16)
            xn_ref[rows, :] = xh
            p = jnp.dot(xh, wcat, preferred_element_type=F32)
            os_ref[rows, :] = (p[:, :LANES] + p[:, LANES:]
                               + jnp.dot(xl, wh, preferred_element_type=F32) + bs_ref[...])

    o_ref[...] = jnp.dot(xn_ref[...], w_ref[...], preferred_element_type=F32)


def _inproj(x, norm_w, w_main, layer, w_small, b_small, *, tm, tn):
    t, d = x.shape
    n = w_main.shape[2]
    return pl.pallas_call(
        _inproj_kernel,
        out_shape=(jax.ShapeDtypeStruct((t, n), F32), jax.ShapeDtypeStruct((t, LANES), F32)),
        grid=(t // tm, n // tn),
        in_specs=[
            pl.BlockSpec((tm, d), lambda i, j: (i, 0)),
            pl.BlockSpec((1, d), lambda i, j: (0, 0)),
            pl.BlockSpec((None, d, tn), lambda i, j: (layer, 0, j)),
            pl.BlockSpec((None, d, LANES), lambda i, j: (layer, 0, 0)),
            pl.BlockSpec((1, LANES), lambda i, j: (0, 0)),
        ],
        out_specs=(pl.BlockSpec((tm, tn), lambda i, j: (i, j)),
                   pl.BlockSpec((tm, LANES), lambda i, j: (i, 0))),
        scratch_shapes=[pltpu.VMEM((tm, d), BF16)],
        compiler_params=_cparams(("parallel", "arbitrary")),
        name="inproj",
    )(x, norm_w, w_main, w_small, b_small)


CAST_COLS = 1024


def _cast_kernel(w_ref, o_ref, *, segs):
    for src, width, dst in segs:
        for off in range(0, width, CAST_COLS):
            n = min(CAST_COLS, width - off)
            o_ref[0, :, dst + off:dst + off + n] = w_ref[0, :, src + off:src + off + n].astype(BF16)


def _cast_weights(w, segs, *, tr):
    nl, rows, cols = w.shape
    n_out = sum(width for _, width, _ in segs)
    return pl.pallas_call(
        functools.partial(_cast_kernel, segs=segs),
        out_shape=jax.ShapeDtypeStruct((nl, rows, n_out), BF16),
        grid=(nl, rows // tr),
        in_specs=[pl.BlockSpec((1, tr, cols), lambda l, i: (l, i, 0))],
        out_specs=pl.BlockSpec((1, tr, n_out), lambda l, i: (l, i, 0)),
        compiler_params=_cparams(("parallel", "parallel")),
        name="cast_weights",
    )(w)


CAST_T_COLS = 1024


def _cast_t_kernel(a_ref, b_ref, o_ref, os_ref, *, shift_from, shift, small):
    j = pl.program_id(1)

    @pl.when(j == 0)
    def _():
        os_ref[0] = jnp.zeros(os_ref.shape[1:], F32)

    lane = lax.broadcasted_iota(jnp.int32, (1, LANES), 1)
    for tile, use_b, lo_, hi_ in small:
        @pl.when(j == tile)
        def _(use_b=use_b, lo_=lo_, hi_=hi_):
            src = b_ref if use_b else a_ref
            os_ref[0] = jnp.where((lane >= lo_) & (lane < hi_),
                                  jnp.transpose(src[0, 0:LANES, :]), os_ref[0])

    def emit(s):
        for k in range(CAST_T_COLS // LANES):
            lo = s + k * LANES
            if lo + LANES <= CAST_T_COLS:
                blk = a_ref[0, lo:lo + LANES, :]
            else:
                blk = jnp.concatenate([a_ref[0, lo:CAST_T_COLS, :],
                                       b_ref[0, 0:lo + LANES - CAST_T_COLS, :]], axis=0)
            o_ref[0, :, k * LANES:(k + 1) * LANES] = jnp.transpose(blk).astype(BF16)

    if shift == 0:
        emit(0)
    else:
        @pl.when(pl.program_id(1) < shift_from)
        def _():
            emit(0)

        @pl.when(pl.program_id(1) >= shift_from)
        def _():
            emit(shift)


def _cast_weights_t(wt, n_out, small, *, shift_from=0, shift=0):
    nl, _, d = wt.shape
    per = CAST_T_COLS // LANES
    return pl.pallas_call(
        functools.partial(_cast_t_kernel, shift_from=shift_from, shift=shift, small=small),
        out_shape=(jax.ShapeDtypeStruct((nl, d, n_out), BF16),
                   jax.ShapeDtypeStruct((nl, d, LANES), F32)),
        grid=(nl, n_out // CAST_T_COLS),
        in_specs=[pl.BlockSpec((1, CAST_T_COLS, d), lambda l, j: (l, j, 0)),
                  pl.BlockSpec((1, LANES, d), lambda l, j: (l, (j + 1) * per, 0))],
        out_specs=(pl.BlockSpec((1, d, CAST_T_COLS), lambda l, j: (l, 0, j)),
                   pl.BlockSpec((1, d, LANES), lambda l, j: (l, 0, 0))),
        compiler_params=_cparams(("parallel", "arbitrary")),
        name="cast_weights_t",
    )(wt, wt)


def _outproj_kernel(a_ref, b_ref, wa_ref, wb_ref, x_ref, o_ref):
    o_ref[...] = (x_ref[...]
                  + jnp.dot(a_ref[...], wa_ref[...], preferred_element_type=F32)
                  + jnp.dot(b_ref[...], wb_ref[...], preferred_element_type=F32))


def _outproj(a, b, a_blk, b_blk, w, layer, x, *, tm, tn):
    t, d = x.shape
    half = D_INNER // 2
    return pl.pallas_call(
        _outproj_kernel,
        out_shape=jax.ShapeDtypeStruct((t, d), F32),
        grid=(t // tm, d // tn),
        in_specs=[
            pl.BlockSpec((tm, half), lambda i, j: (i, a_blk)),
            pl.BlockSpec((tm, half), lambda i, j: (i, b_blk)),
            pl.BlockSpec((None, half, tn), lambda i, j: (layer, 0, j)),
            pl.BlockSpec((None, half, tn), lambda i, j: (layer, 1, j)),
            pl.BlockSpec((tm, tn), lambda i, j: (i, j)),
        ],
        out_specs=pl.BlockSpec((tm, tn), lambda i, j: (i, j)),
        compiler_params=_cparams(("parallel", "arbitrary")),
        name="outproj",
    )(a, b, w, w, x)


def _final_norm_kernel(x_ref, w_ref, o_ref):
    o_ref[...] = _rms(x_ref[...]) * w_ref[...]


def _final_norm(x, w, *, tm):
    t, d = x.shape
    return pl.pallas_call(
        _final_norm_kernel,
        out_shape=jax.ShapeDtypeStruct((t, d), F32),
        grid=(t // tm,),
        in_specs=[pl.BlockSpec((tm, d), lambda i: (i, 0)),
                  pl.BlockSpec((1, d), lambda i: (0, 0))],
        out_specs=pl.BlockSpec((tm, d), lambda i: (i, 0)),
        compiler_params=_cparams(("parallel",)),
        name="final_norm",
    )(x, w)


def _causal_conv(buf_ref, x, w_ref, b_ref, c):
    buf_ref[pl.ds(SUBLANES, c), :] = x
    xp = buf_ref[...]
    out = b_ref[...] + x * w_ref[CONV - 1:CONV, :]
    for s in range(1, CONV):
        out = out + pltpu.roll(xp, s, 0)[SUBLANES:, :] * w_ref[CONV - 1 - s:CONV - s, :]
    buf_ref[pl.ds(SUBLANES - HIST, HIST), :] = buf_ref[pl.ds(SUBLANES + c - HIST, HIST), :]
    return out


def _init_hist(buf_ref, hist):
    buf_ref[pl.ds(SUBLANES - HIST, HIST), :] = hist


def _run_init(first, fn):
    if first is None:
        fn()
    else:
        pl.when(first)(fn)


def _gla_body(q_ref, k_ref, v_ref, z_ref, sm_ref, wa_ref, ba_ref, gn_ref, s0_ref, o_ref, s_ref,
              *, phase, first, c, c_true, valid):
    read_given = first is None and s0_ref is not None
    if phase == "init":
        if not read_given:
            def init():
                if s0_ref is not None:
                    s_ref[0, 0] = s0_ref[0, 0]
                else:
                    s_ref[0, 0] = jnp.zeros((HEADS, DK, DV), F32)
            _run_init(first, init)
        return
    src_ref = s0_ref if read_given else s_ref

    ag = sm_ref[0][:, :GLA_LOWRANK]
    a1 = ag.astype(BF16).astype(F32)
    a2 = (ag - a1).astype(BF16).astype(F32)
    gpre = _dot(jnp.concatenate([a1, a1, a2], axis=1), wa_ref[...]) + ba_ref[...]
    g = _log_sigmoid(gpre) * (1.0 / GLA_GATE_TEMP)
    rv = _row_valid(c, valid) if valid < c else None
    if rv is not None:
        g = jnp.where(rv, g, 0.0)
    tri = _tri(c)
    g3 = jnp.concatenate(_split3(g), axis=0).astype(BF16)
    tri3 = jnp.concatenate([tri.astype(F32)] * 3, axis=1).astype(BF16)
    b = jnp.dot(tri3, g3, preferred_element_type=F32)
    dcol = jnp.exp(lax.dot_general(g3, jnp.ones((3 * c, LANES), BF16), (((0,), (0,)), ((), ())),
                                   preferred_element_type=F32))

    for h in range(HEADS):
        ks = slice(h * DK, (h + 1) * DK)
        vs = slice(h * DV, (h + 1) * DV)
        q = q_ref[0, :, ks] * (DK ** -0.5)
        k = k_ref[0, :, ks]
        if rv is not None:
            k = jnp.where(rv, k, 0.0)
        v = v_ref[0, :, vs]
        bh = b[:, ks]
        mid = bh[c_true // 2:c_true // 2 + 1, :]
        b_last = bh[c - 1:c, :]
        s = src_ref[0, 0, h]
        o = _dot(q * jnp.exp(bh), s)
        att = _dot_nt(q * jnp.exp(bh - mid), k * jnp.exp(mid - bh))
        o = o + _dot(jnp.where(tri, att, 0.0), v)
        dc = dcol[ks, :]
        s_ref[0, 0, h] = (s * jnp.concatenate([dc] * (DV // LANES), axis=1)
                          + _dot_tn(k * jnp.exp(b_last - bh), v))
        o_ref[0, :, vs] = (_rms(o) * gn_ref[:, vs] * _silu(z_ref[0, :, vs])).astype(BF16)


def _mlstm_body(qk_ref, v_ref, og_ref, z_ref, sm_ref, gt_ref, cw_ref, cb_ref, mn_ref, init_refs,
                o_ref, c_ref, n_ref, m_ref, buf, *, phase, first, c, valid):
    has_init = init_refs is not None
    if has_init:
        c0_ref, n0_ref, m0_ref, h0_ref = init_refs
    read_given = first is None and has_init
    if phase == "init":
        def init():
            if has_init:
                if not read_given:
                    c_ref[0, 0] = c0_ref[0, 0]
                    n_ref[0, 0] = n0_ref[0, 0]
                    m_ref[0, 0] = jnp.broadcast_to(m0_ref[0, 0], (HEADS, 1, LANES))
                _init_hist(buf, h0_ref[0, 0])
            else:
                c_ref[0, 0] = jnp.zeros((HEADS, DK, DV), F32)
                n_ref[0, 0] = jnp.zeros((HEADS, 1, DK), F32)
                m_ref[0, 0] = jnp.zeros((HEADS, 1, LANES), F32)
                _init_hist(buf, jnp.zeros((HIST, 2 * QK), F32))
        _run_init(first, init)
        return
    c_src, n_src, m_src = (c0_ref, n0_ref, m0_ref) if read_given else (c_ref, n_ref, m_ref)

    qk = _silu(_causal_conv(buf, qk_ref[0], cw_ref, cb_ref, c))
    sm = sm_ref[0]
    gt = gt_ref[0, 0]
    rv = _row_valid(c, valid) if valid < c else None
    cv = _col_valid(c, valid) if valid < c else None
    tri = _tri(c)
    row = lax.broadcasted_iota(jnp.int32, (c, c), 0)
    col = lax.broadcasted_iota(jnp.int32, (c, c), 1)
    triu = row <= col

    for h in range(HEADS):
        vs = slice(h * DV, (h + 1) * DV)
        q = qk[:, h * DK:(h + 1) * DK]
        k = qk[:, QK + h * DK:QK + (h + 1) * DK] * (DK ** -0.5)
        v = v_ref[0, :, vs]
        ig_c = sm[:, SM_I + h:SM_I + h + 1]
        lf_c = _log_sigmoid(sm[:, SM_F + h:SM_F + h + 1])
        ig_r = gt[h:h + 1, :]
        lf_r = _log_sigmoid(gt[HEADS + h:HEADS + h + 1, :])
        if rv is not None:
            lf_c = jnp.where(rv, lf_c, 0.0)
            lf_r = jnp.where(cv, lf_r, 0.0)
            ig_c = jnp.where(rv, ig_c, NEG)
            ig_r = jnp.where(cv, ig_r, NEG)
        fcum_c = jnp.sum(jnp.where(tri, lf_r, 0.0), axis=1, keepdims=True)
        fcum_r = jnp.sum(jnp.where(triu, lf_c, 0.0), axis=0, keepdims=True)
        m_prev = m_src[0, 0, h][:, 0:1]
        dlog = jnp.where(tri, fcum_c - fcum_r + ig_r, NEG)
        inter = fcum_c + m_prev
        m_i = jnp.maximum(inter, jnp.max(dlog, axis=1, keepdims=True))
        w_inter = jnp.exp(inter - m_i)
        qkm = _dot_nt(q, k) * jnp.exp(dlog - m_i)
        cm = c_src[0, 0, h]
        nm = n_src[0, 0, h]
        num = _dot(qkm, v) + w_inter * _dot(q, cm)
        den = (jnp.sum(qkm, axis=1, keepdims=True)
               + w_inter * jnp.sum(q * nm, axis=1, keepdims=True))
        den = jnp.maximum(jnp.abs(den), jnp.exp(-m_i))
        hh = num / den
        m_new = m_i[c - 1:c, :]
        f_last = fcum_c[c - 1:c, :]
        w_j = jnp.exp(f_last - fcum_c + ig_c - m_new)
        decay = jnp.exp(f_last + m_prev - m_new)
        kw = w_j * k
        c_ref[0, 0, h] = decay * cm + _dot_tn(kw, v)
        n_ref[0, 0, h] = decay * nm + jnp.sum(kw, axis=0, keepdims=True)
        m_ref[0, 0, h] = jnp.broadcast_to(m_new, (1, LANES))
        hm = _sigmoid(og_ref[0, :, vs]) * hh
        o_ref[0, :, WIDTH + h * DV:WIDTH + (h + 1) * DV] = (
            _rms(hm) * mn_ref[:, vs] * _silu(z_ref[0, :, vs])).astype(BF16)


N_AB_W = 6
N_AB_ROW = 10
N_AB_INIT = 5
N_AB_STATE_OUT = 4


class _Group(NamedTuple):
    nb: int
    c: int
    c_true: int
    valid: int
    chunked: bool
    has_init: bool
    has_prev: bool


def _ab_kernel(*refs, groups):
    refs = list(refs)
    wa_ref, ba_ref, gn_ref, cw_ref, cb_ref, mn_ref = refs[:N_AB_W]
    refs = refs[N_AB_W:]
    ins = []
    for g in groups:
        rows, refs = refs[:N_AB_ROW], refs[N_AB_ROW:]
        init = []
        if g.has_init:
            init, refs = refs[:N_AB_INIT], refs[N_AB_INIT:]
        if g.has_prev:
            refs = refs[N_AB_STATE_OUT:]
        ins.append((rows, init))
    outs = []
    for g in groups:
        outs.append(refs[:1 + N_AB_STATE_OUT])
        refs = refs[1 + N_AB_STATE_OUT:]
    for phase in ("init", "compute"):
        for g, (rows, init), out, buf in zip(groups, ins, outs, refs):
            q_ref, k_ref, v_ref, z_ref, sm_ref, qkm_ref, vm_ref, og_ref, zm_ref, gt_ref = rows
            o_ref, s_ref, c_ref, n_ref, m_ref = out
            first = pl.program_id(1) == 0 if g.chunked else None
            for bi in range(g.nb):
                seq = slice(bi, bi + 1)
                row = lambda r: r.at[seq]
                st = lambda r: r.at[:, seq]
                s0_ref = st(init[0]) if g.has_init else None
                init_refs = [st(r) for r in init[1:]] if g.has_init else None
                _gla_body(row(q_ref), row(k_ref), row(v_ref), row(z_ref), row(sm_ref), wa_ref,
                          ba_ref, gn_ref, s0_ref, row(o_ref), st(s_ref), phase=phase, first=first,
                          c=g.c, c_true=g.c_true, valid=g.valid)
                _mlstm_body(row(qkm_ref), row(vm_ref), row(og_ref), row(zm_ref), row(sm_ref),
                            row(gt_ref), cw_ref, cb_ref, mn_ref, init_refs, row(o_ref), st(c_ref),
                            st(n_ref), st(m_ref), buf.at[bi], phase=phase, first=first, c=g.c,
                            valid=g.valid)


def _ab_scan(data, P, *, layer, n_layers):
    d0 = data[0]
    grid = (d0['main'].shape[0] // d0['nb'], d0['main'].shape[1] // d0['c'])
    base = (2 * QK + 2 * WIDTH) // WIDTH
    const = lambda b, n: (0, 0)
    in_specs = [
        pl.BlockSpec((3 * GLA_LOWRANK, QK), const),
        pl.BlockSpec((1, QK), const),
        pl.BlockSpec((1, WIDTH), const),
        pl.BlockSpec((CONV, 2 * QK), const),
        pl.BlockSpec((1, 2 * QK), const),
        pl.BlockSpec((1, WIDTH), const),
    ]
    args = [P['wa3'], P['ba'], P['gnorm'], P['cw'], P['cb'], P['mnorm']]
    groups, out_shape, out_specs, scratch, aliases = [], [], [], [], {}
    n_out = 0
    for gi, d in enumerate(data):
        main, nb, c = d['main'], d['nb'], d['c']
        bsz, L, _ = main.shape
        chunked = gi == 0
        if chunked:
            seq = lambda b, n: b
            chunk = lambda b, n: n
        else:
            assert L == c and bsz // nb == grid[0] * grid[1]
            seq = lambda b, n: b * grid[1] + n
            chunk = lambda b, n: 0
        row = lambda blk, seq=seq, chunk=chunk: (lambda b, n: (seq(b, n), chunk(b, n), blk))
        state = lambda b, n, seq=seq: (layer, seq(b, n), 0, 0, 0)
        in_specs += [
            pl.BlockSpec((nb, c, QK), row(0)),
            pl.BlockSpec((nb, c, QK), row(1)),
            pl.BlockSpec((nb, c, WIDTH), row(2 * QK // WIDTH)),
            pl.BlockSpec((nb, c, WIDTH), row(2 * QK // WIDTH + 1)),
            pl.BlockSpec((nb, c, LANES), row(0)),
            pl.BlockSpec((nb, c, 2 * QK), row(base)),
            pl.BlockSpec((nb, c, WIDTH), row(base + 1)),
            pl.BlockSpec((nb, c, WIDTH), row(base + 2)),
            pl.BlockSpec((nb, c, WIDTH), row(base + 3)),
            pl.BlockSpec((nb, 1, 2 * HEADS, c),
                         lambda b, n, seq=seq, chunk=chunk: (seq(b, n), chunk(b, n), 0, 0)),
        ]
        args += [main, main, main, main, d['small'], main, main, main, main, d['gates_t']]
        has_init = d['init'] is not None
        has_prev = d['prev'] is not None
        if has_init:
            in_specs += [
                pl.BlockSpec((1, nb, HEADS, DK, DV), state),
                pl.BlockSpec((1, nb, HEADS, DK, DV), state),
                pl.BlockSpec((1, nb, HEADS, 1, DK), state),
                pl.BlockSpec((1, nb, HEADS, 1, 1), state),
                pl.BlockSpec((1, nb, HIST, 2 * QK), lambda b, n, seq=seq: (layer, seq(b, n), 0, 0)),
            ]
            args += list(d['init'])
        if has_prev:
            for i, p in enumerate(d['prev']):
                aliases[len(args)] = n_out + 1 + i
                in_specs.append(pl.BlockSpec(memory_space=pl.ANY))
                args.append(p)
        out_shape += [jax.ShapeDtypeStruct((bsz, L, 2 * WIDTH), BF16),
                      jax.ShapeDtypeStruct((n_layers, bsz, HEADS, DK, DV), F32),
                      jax.ShapeDtypeStruct((n_layers, bsz, HEADS, DK, DV), F32),
                      jax.ShapeDtypeStruct((n_layers, bsz, HEADS, 1, DK), F32),
                      jax.ShapeDtypeStruct((n_layers, bsz, HEADS, 1, LANES), F32)]
        out_specs += [pl.BlockSpec((nb, c, 2 * WIDTH), row(0)),
                      pl.BlockSpec((1, nb, HEADS, DK, DV), state),
                      pl.BlockSpec((1, nb, HEADS, DK, DV), state),
                      pl.BlockSpec((1, nb, HEADS, 1, DK), state),
                      pl.BlockSpec((1, nb, HEADS, 1, LANES), state)]
        n_out += 1 + N_AB_STATE_OUT
        scratch.append(pltpu.VMEM((nb, SUBLANES + c, 2 * QK), F32))
        groups.append(_Group(nb=nb, c=c, c_true=d['c_true'], valid=d['valid'], chunked=chunked,
                             has_init=has_init, has_prev=has_prev))
    res = pl.pallas_call(
        functools.partial(_ab_kernel, groups=tuple(groups)),
        out_shape=tuple(out_shape),
        grid=grid,
        in_specs=in_specs,
        out_specs=tuple(out_specs),
        scratch_shapes=scratch,
        input_output_aliases=aliases,
        compiler_params=_cparams(("parallel", "arbitrary")),
        name="ab_scan",
    )(*args)
    per = 1 + N_AB_STATE_OUT
    return [res[i * per:(i + 1) * per] for i in range(len(data))]


def _pad_rows(a, rows):
    if a.shape[0] == rows:
        return a
    return jnp.concatenate([a, jnp.zeros((rows - a.shape[0], a.shape[1]), a.dtype)], axis=0)


N_SSD_W = 7
N_SSD_ROW = 4
N_SSD_INIT = 3


class _SsdGroup(NamedTuple):
    nb: int
    c: int
    valid: int
    chunked: bool
    has_init: bool
    has_prev: bool
    transposed: bool
    last: int


def _ssd_kernel(*refs, groups):
    refs = list(refs)
    weights, refs = refs[:N_SSD_W], refs[N_SSD_W:]
    ins = []
    for g in groups:
        rows, refs = refs[:N_SSD_ROW], refs[N_SSD_ROW:]
        init = []
        if g.has_init:
            init, refs = refs[:N_SSD_INIT], refs[N_SSD_INIT:]
        if g.has_prev:
            refs = refs[1:]
        ins.append((rows, init))
    outs = []
    for g in groups:
        outs.append(refs[:2])
        refs = refs[2:]
    scratch = []
    for g in groups:
        n = 3 if g.transposed else 2
        scratch.append(refs[:n])
        refs = refs[n:]
    for phase in ("init", "compute", "final"):
        for g, (rows, init), (o_ref, s_ref), scr in zip(groups, ins, outs, scratch):
            first = pl.program_id(1) == 0 if g.chunked else None
            for bi in range(g.nb):
                seq = slice(bi, bi + 1)
                _ssd_body(*[r.at[seq] for r in rows], *weights, [r.at[:, seq] for r in init],
                          o_ref.at[seq], s_ref.at[:, seq], *[r.at[bi] for r in scr], phase=phase,
                          first=first, c=g.c, valid=g.valid, transposed=g.transposed, last=g.last)


def _ssd_body(z_ref, xp_ref, bcp_ref, sm_ref, cwx_ref, cwbc_ref, cbx_ref, cbbc_ref,
              alog_ref, d_ref, nw_ref, init_refs, o_ref, s_ref, xbuf, bcbuf, st_ref=None,
              *, phase, first, c, valid, transposed, last):
    has_init = bool(init_refs)
    if has_init:
        s0_ref, hx_ref, hbc_ref = init_refs
    read_given = first is None and has_init and not transposed
    if phase == "init":
        def init():
            if has_init:
                _init_hist(xbuf, hx_ref[0, 0])
                _init_hist(bcbuf, hbc_ref[0, 0])
                if transposed:
                    for g in range(SSD_G):
                        st_ref[g] = jnp.transpose(s0_ref[0, 0, g])
                elif not read_given:
                    s_ref[0, 0] = s0_ref[0, 0]
            else:
                _init_hist(xbuf, jnp.zeros((HIST, D_INNER), F32))
                _init_hist(bcbuf, jnp.zeros((HIST, SSD_BC), F32))
                if transposed:
                    st_ref[...] = jnp.zeros((SSD_G, SSD_N, SSD_GW), F32)
                else:
                    s_ref[0, 0] = jnp.zeros((SSD_G, SSD_GW, SSD_N), F32)
        _run_init(first, init)
        return
    if phase == "final":
        if transposed:
            @pl.when(pl.program_id(1) == last)
            def _():
                for g in range(SSD_G):
                    s_ref[0, 0, g] = jnp.transpose(st_ref[g])
        return
    src_ref = s0_ref if read_given else s_ref

    xall = _silu(_causal_conv(xbuf, xp_ref[0], cwx_ref, cbx_ref, c))
    bcall = _silu(_causal_conv(bcbuf, bcp_ref[0], cwbc_ref, cbbc_ref, c))
    dt_all = _softplus(sm_ref[0][:, :SSD_HEADS])
    if valid < c:
        dt_all = jnp.where(_row_valid(c, valid), dt_all, 0.0)
    a_all = -jnp.exp(alog_ref[...])
    tri = _tri(c).astype(F32)
    acum_all = _dot_m01(tri, dt_all * a_all)

    slot = min(c, SLOT)
    slot_log2 = slot.bit_length() - 1
    hw = SSD_HPG * slot
    per_mm = min(SSD_HPG, max(1, LANES // slot))
    iota = lambda shape, axis: lax.broadcasted_iota(jnp.int32, shape, axis)
    e_p = (lax.shift_right_logical(iota((SSD_HPG, SSD_GW), 1), SLOT_LOG2)
           == iota((SSD_HPG, SSD_GW), 0)).astype(F32)
    if slot == SSD_P:
        e = e_p
    else:
        e_j = (lax.shift_right_logical(iota((SSD_HPG, hw), 1), slot_log2)
               == iota((SSD_HPG, hw), 0)).astype(F32)
        e = jnp.concatenate([e_p, e_j], axis=1)
    jpos = jnp.bitwise_and(iota((c, hw), 1), slot - 1)
    causal = jpos <= iota((c, hw), 0)
    diag = jpos == iota((c, hw), 0)
    sel = (lax.shift_right_logical(iota((per_mm * slot, per_mm * SSD_P), 0), slot_log2)
           == lax.shift_right_logical(iota((per_mm * slot, per_mm * SSD_P), 1), SLOT_LOG2))
    tile_dtype = BF16 if slot % (2 * SUBLANES) == 0 else F32

    for g in range(SSD_G):
        gs = slice(g * SSD_GW, (g + 1) * SSD_GW)
        hs = slice(g * SSD_HPG, (g + 1) * SSD_HPG)
        x = xall[:, gs]
        bm = bcall[:, g * SSD_N:(g + 1) * SSD_N]
        cm = bcall[:, SSD_G * SSD_N + g * SSD_N:SSD_G * SSD_N + (g + 1) * SSD_N]
        ex = _dot_x01(jnp.concatenate([acum_all[:, hs], dt_all[:, hs]], axis=0), e)
        a_exp = ex[:c, :SSD_GW]
        dt_exp = ex[c:, :SSD_GW]
        a_j = a_exp if slot == SSD_P else ex[:c, SSD_GW:]
        a_row = jnp.sum(jnp.where(diag, a_j, 0.0), axis=0, keepdims=True)
        dec = jnp.exp(jnp.where(causal, a_j - a_row, NEG))
        bm_t = jnp.concatenate([_pad_rows(bm, slot)] * SSD_HPG, axis=0)
        m = (_dot_nt(cm, bm_t) * dec).astype(BF16)
        xdt = _pad_rows(x * dt_exp, slot).astype(tile_dtype)
        ys = []
        for pr in range(SSD_HPG // per_mm):
            xp = xdt[:, pr * per_mm * SSD_P:(pr + 1) * per_mm * SSD_P]
            xblk = jnp.where(sel, jnp.concatenate([xp] * per_mm, axis=0), 0.0).astype(BF16)
            ys.append(jnp.dot(m[:, pr * per_mm * slot:(pr + 1) * per_mm * slot], xblk,
                              preferred_element_type=F32))
        y = ys[0] if len(ys) == 1 else jnp.concatenate(ys, axis=1)
        a_last = a_exp[c - 1:c, :]
        wx = x * (jnp.exp(a_last - a_exp) * dt_exp)
        if transposed:
            st = st_ref[g]
            y = y + _dot(cm, st) * jnp.exp(a_exp)
            st_ref[g] = st * jnp.exp(a_last) + _dot_tn(bm, wx)
        else:
            s = src_ref[0, 0, g]
            y = y + _dot_nt(cm, s) * jnp.exp(a_exp)
            upd = _dot_tn(wx, bm)
            sdec = jnp.exp(acum_all[c - 1:c, hs])
            for hh in range(SSD_HPG):
                rs = slice(hh * SSD_P, (hh + 1) * SSD_P)
                s_ref[0, 0, g, rs, :] = s[rs, :] * sdec[:, hh:hh + 1] + upd[rs, :]
        y = y + d_ref[:, gs] * x
        y = y * _silu(z_ref[0, :, gs])
        o_ref[0, :, gs] = (_rms(y) * nw_ref[:, gs]).astype(BF16)


def _ssd_scan(data, P, *, layer, n_layers):
    d0 = data[0]
    grid = (d0['main'].shape[0] // d0['nb'], d0['main'].shape[1] // d0['c'])
    bc_blk = 2 * D_INNER // SSD_BC
    bc_w = D_INNER // SSD_BC
    const = lambda blk: (lambda b, n: (0, blk))
    in_specs = [
        pl.BlockSpec((CONV, D_INNER), const(0)),
        pl.BlockSpec((CONV, SSD_BC), const(bc_w)),
        pl.BlockSpec((1, D_INNER), const(0)),
        pl.BlockSpec((1, SSD_BC), const(bc_w)),
        pl.BlockSpec((1, SSD_HEADS), const(0)),
        pl.BlockSpec((1, D_INNER), const(0)),
        pl.BlockSpec((1, D_INNER), const(0)),
    ]
    args = [P['cw'], P['cw'], P['cb'], P['cb'], P['alog'], P['d_exp'], P['normw']]
    groups, out_shape, out_specs, scratch, aliases = [], [], [], [], {}
    for gi, d in enumerate(data):
        main, nb, c = d['main'], d['nb'], d['c']
        bsz, L, _ = main.shape
        chunked = gi == 0
        if chunked:
            seq = lambda b, n: b
            chunk = lambda b, n: n
        else:
            assert L == c and bsz // nb == grid[0] * grid[1]
            seq = lambda b, n: b * grid[1] + n
            chunk = lambda b, n: 0
        row = lambda blk, seq=seq, chunk=chunk: (lambda b, n: (seq(b, n), chunk(b, n), blk))
        state = lambda b, n, seq=seq: (layer, seq(b, n), 0, 0, 0)
        hist = lambda blk, seq=seq: (lambda b, n: (layer, seq(b, n), 0, blk))
        in_specs += [
            pl.BlockSpec((nb, c, D_INNER), row(0)),
            pl.BlockSpec((nb, c, D_INNER), row(1)),
            pl.BlockSpec((nb, c, SSD_BC), row(bc_blk)),
            pl.BlockSpec((nb, c, LANES), row(0)),
        ]
        args += [main, main, main, d['small']]
        has_init = d['init'] is not None
        has_prev = d['prev'] is not None
        if has_init:
            s0, conv0 = d['init']
            in_specs += [
                pl.BlockSpec((1, nb, SSD_G, SSD_GW, SSD_N), state),
                pl.BlockSpec((1, nb, HIST, D_INNER), hist(0)),
                pl.BlockSpec((1, nb, HIST, SSD_BC), hist(bc_w)),
            ]
            args += [s0, conv0, conv0]
        if has_prev:
            aliases[len(args)] = 2 * gi + 1
            in_specs.append(pl.BlockSpec(memory_space=pl.ANY))
            args.append(d['prev'])
        out_shape += [jax.ShapeDtypeStruct((bsz, L, D_INNER), BF16),
                      jax.ShapeDtypeStruct((n_layers, bsz, SSD_G, SSD_GW, SSD_N), F32)]
        out_specs += [pl.BlockSpec((nb, c, D_INNER), row(0)),
                      pl.BlockSpec((1, nb, SSD_G, SSD_GW, SSD_N), state)]
        scratch += [pltpu.VMEM((nb, SUBLANES + c, D_INNER), F32),
                    pltpu.VMEM((nb, SUBLANES + c, SSD_BC), F32)]
        if d['transposed']:
            scratch.append(pltpu.VMEM((nb, SSD_G, SSD_N, SSD_GW), F32))
        groups.append(_SsdGroup(nb=nb, c=c, valid=d['valid'], chunked=chunked, has_init=has_init,
                                has_prev=has_prev, transposed=d['transposed'], last=L // c - 1))
    res = pl.pallas_call(
        functools.partial(_ssd_kernel, groups=tuple(groups)),
        out_shape=tuple(out_shape),
        grid=grid,
        in_specs=in_specs,
        out_specs=tuple(out_specs),
        scratch_shapes=scratch,
        input_output_aliases=aliases,
        compiler_params=_cparams(("parallel", "arbitrary")),
        name="ssd_scan",
    )(*args)
    return [res[2 * i:2 * i + 2] for i in range(len(data))]


def _prep_params(p):
    n_ab = p['w_in_ab'].shape[0]
    n_c = p['w_in_c'].shape[0]
    o_ag = 2 * QK + 2 * WIDTH
    o_m = o_ag + GLA_LOWRANK
    o_if = o_m + 2 * QK + 3 * WIDTH
    ab, cc = [], []
    wt_ab = jnp.swapaxes(p['w_in_ab'], 1, 2)
    wt_c = jnp.swapaxes(p['w_in_c'], 1, 2)
    n_main_ab = o_if - GLA_LOWRANK
    o_dt = D_INNER + SSD_CONV_DIM
    w_main_ab, w_small_ab = _cast_weights_t(
        wt_ab, n_main_ab,
        ((o_ag // CAST_T_COLS, False, 0, GLA_LOWRANK),
         (n_main_ab // CAST_T_COLS - 1, True, SM_I, SM_I + 2 * HEADS)),
        shift_from=o_ag // CAST_T_COLS, shift=GLA_LOWRANK)
    w_main_c, w_small_c = _cast_weights_t(
        wt_c, o_dt, ((o_dt // CAST_T_COLS - 1, True, 0, SSD_HEADS),))
    w_out_ab = _cast_weights(p['w_out_ab'], ((0, D_MODEL, 0),), tr=512)
    w_out_c = _cast_weights(p['w_out_c'], ((0, D_MODEL, 0),), tr=512)
    for i in range(n_ab):
        b_small = jnp.zeros((1, LANES), F32)
        b_small = b_small.at[0, SM_I:SM_I + HEADS].set(p['mlstm_b_i'][i])
        b_small = b_small.at[0, SM_F:SM_F + HEADS].set(p['mlstm_b_f'][i])
        wa = p['gla_w_a2'][i]
        wa_hi = wa.astype(BF16)
        wa_lo = (wa - wa_hi.astype(F32)).astype(BF16)
        ab.append(dict(
            norm=p['norm_ab'][i][None], w_main=w_main_ab, w_small=w_small_ab, b_small=b_small,
            wa3=jnp.concatenate([wa_hi, wa_lo, wa_hi], axis=0),
            ba=p['gla_b_a'][i][None], gnorm=p['gla_norm'][i][None],
            cw=p['mlstm_conv_w'][i], cb=p['mlstm_conv_b'][i][None],
            mnorm=p['mlstm_norm'][i][None], w_out=w_out_ab))
    for i in range(n_c):
        b_small = jnp.zeros((1, LANES), F32).at[0, :SSD_HEADS].set(p['ssd_dt_bias'][i])
        cc.append(dict(
            norm=p['norm_c'][i][None], w_main=w_main_c, w_small=w_small_c,
            b_small=b_small, cw=p['ssd_conv_w'][i], cb=p['ssd_conv_b'][i][None],
            alog=p['ssd_a_log'][i][None],
            d_exp=jnp.repeat(p['ssd_d'][i], SSD_P)[None],
            normw=p['ssd_norm'][i][None], w_out=w_out_c))
    return ab, cc


def _trunk(xs, cfgs, states, ab, cc, final_w):
    n_ab, n_c = len(ab), len(cc)
    ng = len(xs)
    dims = [x.shape[:2] for x in xs]
    xt = [x.reshape(-1, D_MODEL) for x in xs]
    ab_o = [None] * ng
    ssm_o = [None] * ng
    mconv = [[] for _ in xs]
    sconv = [[] for _ in xs]
    row0 = [L - cfg['c'] + cfg['valid'] - HIST for (_, L), cfg in zip(dims, cfgs)]
    for layer in range(n_ab + n_c):
        i = layer // 2
        P = ab[i] if layer % 2 == 0 else cc[i]
        mains, smalls = [], []
        for g in range(ng):
            main, small = _inproj(xt[g], P['norm'], P['w_main'], i, P['w_small'], P['b_small'],
                                  tm=min(PROJ_TM, xt[g].shape[0]), tn=INPROJ_TN)
            mains.append(main.reshape(dims[g] + (-1,)))
            smalls.append(small.reshape(dims[g] + (LANES,)))
        if layer % 2 == 0:
            data = []
            for g in range(ng):
                (bsz, L), cfg, st = dims[g], cfgs[g], states[g]
                c = cfg['c']
                gates_t = jnp.swapaxes(smalls[g][:, :, SM_I:SM_I + 2 * HEADS].reshape(
                    bsz, L // c, c, 2 * HEADS), 2, 3)
                init = None if st is None else (st['gla'], st['mc'], st['mn'], st['mm'], st['mconv'])
                prev = None if ab_o[g] is None else ab_o[g][1:]
                data.append(dict(main=mains[g], small=smalls[g], gates_t=gates_t, init=init,
                                 prev=prev, nb=cfg['nb'], c=c, c_true=cfg['c_true'],
                                 valid=cfg['valid']))
                mconv[g].append(mains[g][:, row0[g]:row0[g] + HIST,
                                         2 * QK + 2 * WIDTH:2 * QK + 2 * WIDTH + 2 * QK])
            ab_o = _ab_scan(data, P, layer=i, n_layers=n_ab)
            ys = [o[0] for o in ab_o]
        else:
            data = []
            for g in range(ng):
                cfg, st = cfgs[g], states[g]
                init = None if st is None else (st['ssm'], st['sconv'])
                data.append(dict(main=mains[g], small=smalls[g], init=init, prev=ssm_o[g],
                                 nb=cfg['nb'], c=cfg['c'], valid=cfg['valid'],
                                 transposed=cfg['transposed']))
                sconv[g].append(mains[g][:, row0[g]:row0[g] + HIST, D_INNER:D_INNER + SSD_CONV_DIM])
            res = _ssd_scan(data, P, layer=i, n_layers=n_c)
            ys = [r[0] for r in res]
            ssm_o = [r[1] for r in res]
        for g in range(ng):
            y2 = ys[g].reshape(-1, D_INNER)
            xt[g] = _outproj(y2, y2, 0, 1, P['w_out'], i, xt[g],
                             tm=min(PROJ_TM, xt[g].shape[0]), tn=OUTPROJ_TN)
    outs = []
    for g in range(ng):
        bsz, L = dims[g]
        y = _final_norm(xt[g], final_w, tm=min(NORM_TM, xt[g].shape[0])).reshape(bsz, L, D_MODEL)
        _, gla_o, mc_o, mn_o, mm_o = ab_o[g]
        outs.append((y, gla_o, mc_o, mn_o[:, :, :, 0, :], mm_o[:, :, :, 0, 0], jnp.stack(mconv[g]),
                     ssm_o[g].reshape(n_c, bsz, SSD_HEADS, SSD_P, SSD_N), jnp.stack(sconv[g])))
    return outs


def kernel(x_prompt, x_sample, state_gla, state_mlstm_c, state_mlstm_n, state_mlstm_m, state_mlstm_conv, state_ssm, state_ssm_conv, norm_ab, w_in_ab, gla_w_a2, gla_b_a, gla_norm, mlstm_conv_w, mlstm_conv_b, mlstm_b_i, mlstm_b_f, mlstm_norm, w_out_ab, norm_c, w_in_c, ssd_conv_w, ssd_conv_b, ssd_dt_bias, ssd_a_log, ssd_d, ssd_norm, w_out_c, final_norm):
    p = dict(norm_ab=norm_ab, w_in_ab=w_in_ab, gla_w_a2=gla_w_a2, gla_b_a=gla_b_a, gla_norm=gla_norm,
             mlstm_conv_w=mlstm_conv_w, mlstm_conv_b=mlstm_conv_b, mlstm_b_i=mlstm_b_i,
             mlstm_b_f=mlstm_b_f, mlstm_norm=mlstm_norm, w_out_ab=w_out_ab, norm_c=norm_c,
             w_in_c=w_in_c, ssd_conv_w=ssd_conv_w, ssd_conv_b=ssd_conv_b, ssd_dt_bias=ssd_dt_bias,
             ssd_a_log=ssd_a_log, ssd_d=ssd_d, ssd_norm=ssd_norm, w_out_c=w_out_c)
    ab, cc = _prep_params(p)
    fw = final_norm[None]
    n_ab, dec_b = state_gla.shape[0], state_gla.shape[1]
    n_c = state_ssm.shape[0]
    dec_l = x_sample.shape[1]

    xs = jnp.pad(x_sample, ((0, 0), (0, SUBLANES - dec_l), (0, 0)))
    states = dict(
        gla=state_gla, mc=state_mlstm_c,
        mn=state_mlstm_n.reshape(n_ab, dec_b, HEADS, 1, DK),
        mm=state_mlstm_m.reshape(n_ab, dec_b, HEADS, 1, 1),
        mconv=state_mlstm_conv,
        ssm=state_ssm.reshape(n_c, dec_b, SSD_G, SSD_GW, SSD_N),
        sconv=state_ssm_conv)
    prompt_steps = x_prompt.shape[0] * (x_prompt.shape[1] // CHUNK)
    sample_per_step = dec_b // prompt_steps
    cfgs = [dict(c=CHUNK, c_true=CHUNK, valid=CHUNK, transposed=True, nb=1),
            dict(c=SUBLANES, c_true=dec_l, valid=dec_l, transposed=False, nb=sample_per_step)]
    ((yp, gla_p, mc_p, mn_p, mm_p, mcv_p, ssm_p, scv_p),
     (ys, gla_s, mc_s, mn_s, mm_s, mcv_s, ssm_s, scv_s)) = _trunk(
        [x_prompt, xs], cfgs, [None, states], ab, cc, fw)
    ys = ys[:, :dec_l]
    return (yp, ys, gla_p, gla_s, mc_p, mc_s, mn_p, mn_s, mm_p, mm_s,
            mcv_p, mcv_s, ssm_p, ssm_s, scv_p, scv_s)
```
